```python
import math
import jax, jax.numpy as jnp
from jax import lax
import numpy as np

D_MODEL = 1024
BATCH = 8
SEQ = 4096
DEPTH = 1

N_META = 16
GDN_HEADS = 8
GDN_DK = 128
GDN_DV = 128
CONV_WIDTH = 4
CHUNK = 64
ATT_HEADS = 8
ATT_KV_HEADS = 2
ATT_HD = 128
ATT_GROUP = ATT_HEADS // ATT_KV_HEADS
IDX_HEADS = 8
IDX_HD = 64
TOPK_MAX = 256
Q_BLOCK = 128
ROPE_THETA = 10000.0
D_FF = 4 * D_MODEL
EPS = 1e-6

GDN_QK_W = GDN_HEADS * GDN_DK
GDN_V_W = GDN_HEADS * GDN_DV
ATT_Q_W = ATT_HEADS * ATT_HD
ATT_KV_W = ATT_KV_HEADS * ATT_HD
SPLITS = (GDN_QK_W, GDN_QK_W, GDN_V_W, GDN_V_W, GDN_HEADS, GDN_HEADS,
          ATT_Q_W, ATT_KV_W, ATT_KV_W, IDX_HEADS * IDX_HD, IDX_HD, IDX_HEADS,
          D_MODEL, D_MODEL)
N_IN = sum(SPLITS)

kernel_name = "hybrid_gdn_dsa_gated_block"


def _rmsnorm(x, g):
    xf = x.astype(jnp.float32)
    y = xf * lax.rsqrt(jnp.mean(xf * xf, axis=-1, keepdims=True) + EPS)
    return (y * g.astype(jnp.float32)).astype(x.dtype)


def _l2norm(x):
    return x * lax.rsqrt(jnp.sum(x * x, axis=-1, keepdims=True) + EPS)


def _rope_tables(pos, dim):
    inv = ROPE_THETA ** (-jnp.arange(0, dim, 2, dtype=jnp.float32) / dim)
    ang = pos.astype(jnp.float32)[:, None] * inv[None, :]
    return jnp.cos(ang), jnp.sin(ang)


def _apply_rope(x, cos, sin):
    half = x.shape[-1] // 2
    xf = x.astype(jnp.float32)
    x1, x2 = xf[..., :half], xf[..., half:]
    c = cos[None, :, None, :]
    s = sin[None, :, None, :]
    return jnp.concatenate([x1 * c - x2 * s, x2 * c + x1 * s], axis=-1).astype(x.dtype)


def _causal_conv(x, w):
    K = w.shape[0]
    T = x.shape[1]
    xp = jnp.pad(x, ((0, 0), (K - 1, 0), (0, 0)))
    y = xp[:, 0:T] * w[0]
    for j in range(1, K):
        y = y + xp[:, j:j + T] * w[j]
    return y


def _gated_delta_rule(q, k, v, beta, g):
    B_, T_, H, dk = q.shape
    dv = v.shape[-1]
    C = CHUNK
    pad = (-N_META) % C
    tail = (-(T_ + pad)) % C

    def padf(a):
        return jnp.pad(a, [(0, 0), (pad, tail)] + [(0, 0)] * (a.ndim - 2))

    q, k, v, beta, g = padf(q), padf(k), padf(v), padf(beta), padf(g)
    N = (T_ + pad + tail) // C

    def vec_chunks(a):
        return a.reshape(B_, N, C, H, a.shape[-1]).transpose(0, 3, 1, 2, 4)

    def sc_chunks(a):
        return a.reshape(B_, N, C, H).transpose(0, 3, 1, 2)

    q, k, v = vec_chunks(q), vec_chunks(k), vec_chunks(v)
    beta, g = sc_chunks(beta), sc_chunks(g)
    gc = jnp.cumsum(g, axis=-1)

    tri_incl = jnp.tril(jnp.ones((C, C), dtype=bool))
    tri_strict = jnp.tril(jnp.ones((C, C), dtype=bool), -1)
    decay = jnp.exp(jnp.where(tri_incl, gc[..., :, None] - gc[..., None, :], -jnp.inf))

    kb = k * beta[..., None]
    A = jnp.where(tri_strict, jnp.einsum('bhncd,bhnsd->bhncs', kb, k) * decay, 0.0)
    eye = jnp.eye(C, dtype=jnp.float32)
    Tm = lax.linalg.triangular_solve(A + eye, jnp.broadcast_to(eye, A.shape),
                                     left_side=True, lower=True, unit_diagonal=True)
    u = jnp.einsum('bhncs,bhnse->bhnce', Tm, v * beta[..., None])
    w = jnp.einsum('bhncs,bhnsd->bhncd', Tm, kb * jnp.exp(gc)[..., None])
    attn = jnp.einsum('bhncd,bhnsd->bhncs', q, k) * decay
    q_dec = q * jnp.exp(gc)[..., None]
    k_tail = k * jnp.exp(gc[..., -1:] - gc)[..., None]
    g_last = jnp.exp(gc[..., -1])

    def to_scan(a):
        return jnp.moveaxis(a, 2, 0)

    xs = (to_scan(q_dec), to_scan(attn), to_scan(u), to_scan(w), to_scan(k_tail), to_scan(g_last))

    def step(S, inp):
        qe, at, u_i, w_i, kt, gl = inp
        v_new = u_i - jnp.einsum('bhcd,bhde->bhce', w_i, S)
        o = jnp.einsum('bhcd,bhde->bhce', qe, S) + jnp.einsum('bhcs,bhse->bhce', at, v_new)
        S = S * gl[..., None, None] + jnp.einsum('bhcd,bhce->bhde', kt, v_new)
        return S, o

    S0 = jnp.zeros((B_, H, dk, dv), jnp.float32)
    _, o = lax.scan(step, S0, xs)
    o = o.transpose(1, 0, 3, 2, 4).reshape(B_, N * C, H, dv)
    return o[:, pad:pad + T_]


def _gdn_branch(q, k, v, z, b_raw, a_raw, conv_w, a_log, dt_bias, norm_w):
    B_, T_, _ = q.shape
    dt = q.dtype
    qkv = jax.nn.silu(_causal_conv(jnp.concatenate([q, k, v], axis=-1), conv_w))
    q, k, v = jnp.split(qkv, [GDN_QK_W, 2 * GDN_QK_W], axis=-1)
    q = q.reshape(B_, T_, GDN_HEADS, GDN_DK).astype(jnp.float32)
    k = k.reshape(B_, T_, GDN_HEADS, GDN_DK).astype(jnp.float32)
    v = v.reshape(B_, T_, GDN_HEADS, GDN_DV).astype(jnp.float32)
    q = _l2norm(q) * (GDN_DK ** -0.5)
    k = _l2norm(k)
    beta = jax.nn.sigmoid(b_raw.astype(jnp.float32))
    g = -jnp.exp(a_log.astype(jnp.float32)) * jax.nn.softplus(a_raw.astype(jnp.float32) + dt_bias.astype(jnp.float32))
    o = _gated_delta_rule(q, k, v, beta, g)
    zf = z.reshape(B_, T_, GDN_HEADS, GDN_DV).astype(jnp.float32)
    o = _rmsnorm(o, norm_w) * jax.nn.silu(zf)
    return o.reshape(B_, T_, GDN_V_W).astype(dt)


def _dsa_branch(q, k, v, qi, ki, wi, rope_att, rope_idx, n_visible):
    B_, T_, _ = q.shape
    cos_a, sin_a = rope_att
    cos_i, sin_i = rope_idx
    q = _apply_rope(q.reshape(B_, T_, ATT_HEADS, ATT_HD), cos_a, sin_a)
    k = _apply_rope(k.reshape(B_, T_, ATT_KV_HEADS, ATT_HD), cos_a, sin_a)
    v = v.reshape(B_, T_, ATT_KV_HEADS, ATT_HD)
    qi = _apply_rope(qi.reshape(B_, T_, IDX_HEADS, IDX_HD), cos_i, sin_i)
    ki = _apply_rope(ki.reshape(B_, T_, 1, IDX_HD), cos_i, sin_i)[:, :, 0]
    wi = wi * ((IDX_HEADS * IDX_HD) ** -0.5)
    kv = jnp.concatenate([k, v], axis=-1)
    topk = min(TOPK_MAX, n_visible // 4)
    nb = -(-T_ // Q_BLOCK)
    Tq = nb * Q_BLOCK

    def to_blocks(a):
        a = jnp.pad(a, [(0, 0), (0, Tq - T_)] + [(0, 0)] * (a.ndim - 2))
        return jnp.moveaxis(a.reshape((B_, nb, Q_BLOCK) + a.shape[2:]), 1, 0)

    key_pos = jnp.arange(T_, dtype=jnp.int32)

    def block(args):
        qb, qib, wib, posb = args
        s = jnp.einsum('bqhd,bsd->bqhs', qib, ki)
        s = jnp.einsum('bqhs,bqh->bqs', jax.nn.relu(s), wib).astype(jnp.float32)
        s = jnp.where(key_pos[None, None, :] <= posb[None, :, None], s, -jnp.inf)
        _, idx = lax.top_k(s, topk)
        kv_sel = jax.vmap(lambda a, i: a[i])(kv, idx)
        k_sel, v_sel = jnp.split(kv_sel, 2, axis=-1)
        qg = qb.reshape(B_, Q_BLOCK, ATT_KV_HEADS, ATT_GROUP, ATT_HD)
        logits = jnp.einsum('bqgrd,bqkgd->bqgrk', qg, k_sel).astype(jnp.float32) * (ATT_HD ** -0.5)
        valid = (idx <= posb[None, :, None])[:, :, None, None, :]
        p = jax.nn.softmax(jnp.where(valid, logits, -jnp.inf), axis=-1).astype(v.dtype)
        o = jnp.einsum('bqgrk,bqkgd->bqgrd', p, v_sel)
        return o.reshape(B_, Q_BLOCK, ATT_Q_W)

    pos_blocks = jnp.arange(Tq, dtype=jnp.int32).reshape(nb, Q_BLOCK)
    out = lax.map(block, (to_blocks(q), to_blocks(qi), to_blocks(wi), pos_blocks))
    return jnp.moveaxis(out, 0, 1).reshape(B_, Tq, ATT_Q_W)[:, :T_]


def setup_inputs(seed: int = 0) -> dict:
    key = jax.random.key(seed)
    ks = jax.random.split(key, 20)
    f32 = jnp.float32

    def nrm(k, shape, scale):
        return jax.random.normal(k, shape, f32) * scale

    def gain(k, shape):
        return 1.0 + 0.02 * jax.random.normal(k, shape, f32)

    dt = jnp.exp(jax.random.uniform(ks[5], (DEPTH, GDN_HEADS), f32, math.log(1e-3), math.log(1e-1)))
    return {
        "x": jax.random.normal(ks[0], (BATCH, SEQ, D_MODEL), f32),
        "meta_tokens": nrm(ks[1], (N_META, D_MODEL), 1.0),
        "pre_mix_norm": gain(ks[2], (DEPTH, D_MODEL)),
        "w_in": nrm(ks[3], (DEPTH, D_MODEL, N_IN), D_MODEL ** -0.5),
        "conv_w": nrm(ks[4], (DEPTH, CONV_WIDTH, 2 * GDN_QK_W + GDN_V_W), CONV_WIDTH ** -0.5),
        "a_log": jnp.log(jax.random.uniform(ks[6], (DEPTH, GDN_HEADS), f32, 1.0, 16.0)),
        "dt_bias": jnp.log(jnp.expm1(dt)),
        "gdn_norm": gain(ks[7], (DEPTH, GDN_DV)),
        "w_branch_gdn": nrm(ks[8], (DEPTH, GDN_V_W, D_MODEL), GDN_V_W ** -0.5),
        "w_branch_dsa": nrm(ks[9], (DEPTH, ATT_Q_W, D_MODEL), ATT_Q_W ** -0.5),
        "w_out": nrm(ks[10], (DEPTH, D_MODEL, D_MODEL), D_MODEL ** -0.5),
        "post_mix_norm": gain(ks[11], (DEPTH, D_MODEL)),
        "pre_mlp_norm": gain(ks[12], (DEPTH, D_MODEL)),
        "w_up": nrm(ks[13], (DEPTH, D_MODEL, D_FF), D_MODEL ** -0.5),
        "w_down": nrm(ks[14], (DEPTH, D_FF, D_MODEL), D_FF ** -0.5),
        "post_mlp_norm": gain(ks[15], (DEPTH, D_MODEL)),
    }


def reference(x, meta_tokens, pre_mix_norm, w_in, conv_w, a_log, dt_bias, gdn_norm,
              w_branch_gdn, w_branch_dsa, w_out, post_mix_norm, pre_mlp_norm, w_up, w_down,
              post_mlp_norm):
    B_, L, _ = x.shape
    meta = jnp.broadcast_to(meta_tokens.astype(x.dtype)[None], (B_, N_META, D_MODEL))
    h = jnp.concatenate([meta, x], axis=1)
    T_ = h.shape[1]
    pos = jnp.arange(T_, dtype=jnp.int32)
    rope_att = _rope_tables(pos, ATT_HD)
    rope_idx = _rope_tables(pos, IDX_HD)
    split_at = [int(i) for i in np.cumsum(SPLITS)[:-1]]

    for l in range(DEPTH):
        n = _rmsnorm(h, pre_mix_norm[l])
        proj = jnp.einsum('btd,dn->btn', n, w_in[l])
        (gq, gk, gv, gz, gb, ga, aq, ak, av, iq, ik, iw, gate_a, gate_b) = jnp.split(proj, split_at, axis=-1)
        y_gdn = _gdn_branch(gq, gk, gv, gz, gb, ga, conv_w[l], a_log[l], dt_bias[l], gdn_norm[l])
        y_dsa = _dsa_branch(aq, ak, av, iq, ik, iw, rope_att, rope_idx, L)
        merged = (jax.nn.sigmoid(gate_a) * jnp.einsum('btv,vd->btd', y_gdn, w_branch_gdn[l])
                  + jax.nn.sigmoid(gate_b) * jnp.einsum('btv,vd->btd', y_dsa, w_branch_dsa[l]))
        mix = jnp.einsum('btd,de->bte', merged, w_out[l])
        h = h + _rmsnorm(mix, post_mix_norm[l])
        n2 = _rmsnorm(h, pre_mlp_norm[l])
        u = jnp.square(jax.nn.relu(jnp.einsum('btd,df->btf', n2, w_up[l])))
        h = h + _rmsnorm(jnp.einsum('btf,fd->btd', u, w_down[l]), post_mlp_norm[l])

    return h[:, N_META:]
```

```python
import functools
import math

import jax
import jax.numpy as jnp
import numpy as np
from jax import lax
from jax.experimental import pallas as pl
from jax.experimental.pallas import tpu as pltpu

F32 = jnp.float32
BF16 = jnp.bfloat16
I32 = jnp.int32

D_MODEL = 1024
N_META = 16
GDN_HEADS = 8
GDN_DK = 128
GDN_DV = 128
CONV_WIDTH = 4
ATT_HEADS = 8
ATT_KV_HEADS = 2
ATT_HD = 128
ATT_GROUP = ATT_HEADS // ATT_KV_HEADS
IDX_HEADS = 8
IDX_HD = 64
TOPK_MAX = 256
ROPE_THETA = 10000.0
D_FF = 4 * D_MODEL
EPS = 1e-6

GDN_W = GDN_HEADS * GDN_DK
ATT_Q_W = ATT_HEADS * ATT_HD
ATT_KV_W = ATT_KV_HEADS * ATT_HD
IDX_Q_W = IDX_HEADS * IDX_HD

LANES = 128
BLK = 128
PAD_ROWS = BLK - N_META
KEY_TILE = 256
VMEM_LIMIT = 56 * 1024 * 1024

C_GQ, C_GK, C_GV, C_GZ = 0, 1024, 2048, 3072
C_AQ, C_AK, C_AV, C_IQ = 4096, 5120, 5376, 5632
C_GA, C_GB = 6144, 7168
N_MAIN = 8192
S_IK, S_IW, S_BETA, S_DECAY = 0, 64, 72, 80

NEG_BIG = -1e30
FAST_PROBES = 26


def _cparams(sem):
    return pltpu.CompilerParams(dimension_semantics=sem, vmem_limit_bytes=VMEM_LIMIT)


def _split3(a):
    h = a.astype(BF16)
    r = a - h.astype(F32)
    m = r.astype(BF16)
    l = (r - m.astype(F32)).astype(BF16)
    return h, m, l


def _split2(a):
    h = a.astype(BF16)
    return h, (a - h.astype(F32)).astype(BF16)


def _rms(x, g):
    return x * lax.rsqrt(jnp.mean(x * x, axis=-1, keepdims=True) + EPS) * g


def _proj_kernel(h_ref, g_ref, w_ref, ws_ref, o_ref, os_ref, n_ref):
    j = pl.program_id(1)

    @pl.when(j == 0)
    def _():
        n = _rms(h_ref[...], g_ref[...])
        n_ref[...] = n.astype(BF16)
        nh, nl = _split2(n)
        wh, wl = _split2(ws_ref[...])
        os_ref[...] = (jnp.dot(nh, wh, preferred_element_type=F32)
                       + jnp.dot(nl, wh, preferred_element_type=F32)
                       + jnp.dot(nh, wl, preferred_element_type=F32))

    o_ref[...] = jnp.dot(n_ref[...], w_ref[...], preferred_element_type=F32)


def _proj(hp, g, w_main, w_small, tm, tn):
    R = hp.shape[0]
    return pl.pallas_call(
        _proj_kernel,
        grid=(R // tm, N_MAIN // tn),
        in_specs=[
            pl.BlockSpec((tm, D_MODEL), lambda i, j: (i, 0)),
            pl.BlockSpec((1, D_MODEL), lambda i, j: (0, 0)),
            pl.BlockSpec((D_MODEL, tn), lambda i, j: (0, j)),
            pl.BlockSpec((D_MODEL, LANES), lambda i, j: (0, 0)),
        ],
        out_specs=[
            pl.BlockSpec((tm, tn), lambda i, j: (i, j)),
            pl.BlockSpec((tm, LANES), lambda i, j: (i, 0)),
        ],
        out_shape=[
            jax.ShapeDtypeStruct((R, N_MAIN), F32),
            jax.ShapeDtypeStruct((R, LANES), F32),
        ],
        scratch_shapes=[pltpu.VMEM((tm, D_MODEL), BF16)],
        compiler_params=_cparams(("parallel", "arbitrary")),
        name="in_proj",
    )(hp, g, w_main, w_small)


def _gdn_prep_kernel(x_ref, xp_ref, s_ref, cw_ref, alog_ref, dtb_ref,
                     q_ref, k_ref, v_ref, gb_ref, buf_ref):
    n = pl.program_id(1)
    prev = xp_ref[0]
    buf_ref[0:8, :] = jnp.where(n == 0, jnp.zeros_like(prev), prev)
    buf_ref[8:8 + BLK, :] = x_ref[0]
    y = buf_ref[pl.ds(8 - (CONV_WIDTH - 1), BLK), :] * cw_ref[0:1, :]
    for t in range(1, CONV_WIDTH):
        y = y + buf_ref[pl.ds(8 - (CONV_WIDTH - 1) + t, BLK), :] * cw_ref[t:t + 1, :]
    y = y * jax.nn.sigmoid(y)
    for h in range(GDN_HEADS):
        sl = slice(h * GDN_DK, (h + 1) * GDN_DK)
        q = y[:, C_GQ + h * GDN_DK:C_GQ + (h + 1) * GDN_DK]
        k = y[:, C_GK + h * GDN_DK:C_GK + (h + 1) * GDN_DK]
        q_ref[0, :, sl] = q * lax.rsqrt(jnp.sum(q * q, axis=-1, keepdims=True) + EPS) * (GDN_DK ** -0.5)
        k_ref[0, :, sl] = k * lax.rsqrt(jnp.sum(k * k, axis=-1, keepdims=True) + EPS)
    v_ref[0] = y[:, C_GV:C_GV + GDN_W]

    s = s_ref[0]
    row = lax.broadcasted_iota(I32, (BLK, LANES), 0) + n * BLK
    lane = lax.broadcasted_iota(I32, (BLK, LANES), 1)
    x = s + dtb_ref[...]
    softplus = jnp.maximum(x, 0.0) + jnp.log(1.0 + jnp.exp(-jnp.abs(x)))
    decay = -jnp.exp(alog_ref[...]) * softplus
    beta = jax.nn.sigmoid(s)
    out = jnp.where((lane >= S_DECAY) & (lane < S_DECAY + GDN_HEADS), decay,
                    jnp.where((lane >= S_BETA) & (lane < S_BETA + GDN_HEADS), beta, 0.0))
    gb_ref[0] = jnp.where(row >= PAD_ROWS, out, 0.0)


def _gdn_prep(P, Ps, conv_w, alog_v, dtb_v):
    B, TP, _ = P.shape
    nb = TP // BLK
    wq = 3 * GDN_W
    return pl.pallas_call(
        _gdn_prep_kernel,
        grid=(B, nb),
        in_specs=[
            pl.BlockSpec((1, BLK, wq), lambda b, n: (b, n, 0)),
            pl.BlockSpec((1, 8, wq), lambda b, n: (b, jnp.maximum(n * (BLK // 8) - 1, 0), 0)),
            pl.BlockSpec((1, BLK, LANES), lambda b, n: (b, n, 0)),
            pl.BlockSpec((CONV_WIDTH, wq), lambda b, n: (0, 0)),
            pl.BlockSpec((1, LANES), lambda b, n: (0, 0)),
            pl.BlockSpec((1, LANES), lambda b, n: (0, 0)),
        ],
        out_specs=[
            pl.BlockSpec((1, BLK, GDN_W), lambda b, n: (b, n, 0)),
            pl.BlockSpec((1, BLK, GDN_W), lambda b, n: (b, n, 0)),
            pl.BlockSpec((1, BLK, GDN_W), lambda b, n: (b, n, 0)),
            pl.BlockSpec((1, BLK, LANES), lambda b, n: (b, n, 0)),
        ],
        out_shape=[
            jax.ShapeDtypeStruct((B, TP, GDN_W), F32),
            jax.ShapeDtypeStruct((B, TP, GDN_W), F32),
            jax.ShapeDtypeStruct((B, TP, GDN_W), F32),
            jax.ShapeDtypeStruct((B, TP, LANES), F32),
        ],
        scratch_shapes=[pltpu.VMEM((8 + BLK, wq), F32)],
        compiler_params=_cparams(("parallel", "arbitrary")),
        name="gdn_prep",
    )(P, P, Ps, conv_w, alog_v, dtb_v)


def _mm3(a_parts, b_parts):
    ah, al = a_parts
    bh, bl = b_parts
    lhs = jnp.concatenate([ah, al, ah], axis=1)
    rhs = jnp.concatenate([bh, bh, bl], axis=0)
    return jnp.dot(lhs, rhs, preferred_element_type=F32)


def _unit_lower_inverse(a, eye):
    p = _mm3(_split2(a), _split2(a))
    x = eye - a
    for _ in range(5):
        xs, ps = _split2(x), _split2(p)
        lhs = (jnp.concatenate([xs[0], ps[0]], axis=0), jnp.concatenate([xs[1], ps[1]], axis=0))
        r = _mm3(lhs, ps)
        x = x + r[:BLK]
        p = r[BLK:]
    return x + _mm3(_split2(x), _split2(p))


def _gdn_kernel(q_ref, k_ref, v_ref, z_ref, gb_ref, nw_ref, o_ref, s_ref):
    n = pl.program_id(1)

    @pl.when(n == 0)
    def _():
        s_ref[...] = jnp.zeros_like(s_ref)

    ri = lax.broadcasted_iota(I32, (BLK, BLK), 0)
    ci = lax.broadcasted_iota(I32, (BLK, BLK), 1)
    tri_incl = ri >= ci
    tri_strict = ri > ci
    eye = (ri == ci).astype(F32)
    ones_l = tri_incl.astype(BF16)

    gb = gb_ref[0]
    gh, gm, gl = _split3(gb)
    gc = jnp.dot(jnp.concatenate([ones_l, ones_l, ones_l], axis=1),
                 jnp.concatenate([gh, gm, gl], axis=0), preferred_element_type=F32)
    sel = (lax.broadcasted_iota(I32, (16, LANES), 1)
           == lax.broadcasted_iota(I32, (16, LANES), 0) + S_DECAY).astype(BF16)
    ch, cm, cl = _split3(gc)
    gct = lax.dot_general(jnp.concatenate([sel, sel, sel], axis=1),
                          jnp.concatenate([ch, cm, cl], axis=1),
                          (((1,), (1,)), ((), ())), preferred_element_type=F32)

    for h in range(GDN_HEADS):
        sl = slice(h * GDN_DK, (h + 1) * GDN_DK)
        q = q_ref[0, :, sl]
        k = k_ref[0, :, sl]
        v = v_ref[0, :, sl]
        beta = gb[:, S_BETA + h:S_BETA + h + 1]
        gcc = gc[:, S_DECAY + h:S_DECAY + h + 1]
        gcr = gct[h:h + 1, :]
        g_last = gcc[BLK - 1:BLK, :]
        decay = jnp.where(tri_incl, jnp.exp(jnp.minimum(gcc - gcr, 0.0)), 0.0)
        e_gc = jnp.exp(gcc)
        kb = k * beta
        kbf = k.astype(BF16)
        kk = lax.dot_general(kb.astype(BF16), kbf, (((1,), (1,)), ((), ())), preferred_element_type=F32)
        a = jnp.where(tri_strict, kk * decay, 0.0)
        tm = _unit_lower_inverse(a, eye)
        rhs = jnp.concatenate([v * beta, kb * e_gc], axis=1).astype(BF16)
        uw = jnp.dot(tm.astype(BF16), rhs, preferred_element_type=F32)
        u = uw[:, :GDN_DV]
        w = uw[:, GDN_DV:]
        qk = lax.dot_general(q.astype(BF16), kbf, (((1,), (1,)), ((), ())), preferred_element_type=F32)
        attn = qk * decay
        s_old = s_ref[h]
        s_bf = s_old.astype(BF16)
        v_new = u - jnp.dot(w.astype(BF16), s_bf, preferred_element_type=F32)
        vn_bf = v_new.astype(BF16)
        o = (jnp.dot((q * e_gc).astype(BF16), s_bf, preferred_element_type=F32)
             + jnp.dot(attn.astype(BF16), vn_bf, preferred_element_type=F32))
        k_tail = k * jnp.exp(g_last - gcc)
        s_ref[h] = s_old * jnp.exp(g_last) + jnp.dot(k_tail.T.astype(BF16), vn_bf, preferred_element_type=F32)

        z = z_ref[0, :, sl]
        on = _rms(o, nw_ref[...])
        o_ref[0, :, sl] = (on * (z * jax.nn.sigmoid(z))).astype(o_ref.dtype)


def _gdn(qn, kn, vn, P, gb, norm_w):
    B, TP, _ = qn.shape
    nb = TP // BLK
    L = TP - BLK
    blk = lambda b, n: (b, n, 0)
    return pl.pallas_call(
        _gdn_kernel,
        grid=(B, nb),
        in_specs=[
            pl.BlockSpec((1, BLK, GDN_W), blk),
            pl.BlockSpec((1, BLK, GDN_W), blk),
            pl.BlockSpec((1, BLK, GDN_W), blk),
            pl.BlockSpec((1, BLK, GDN_W), lambda b, n: (b, n, C_GZ // GDN_W)),
            pl.BlockSpec((1, BLK, LANES), blk),
            pl.BlockSpec((1, GDN_DV), lambda b, n: (0, 0)),
        ],
        out_specs=pl.BlockSpec((1, BLK, GDN_W), lambda b, n: (b, jnp.maximum(n - 1, 0), 0)),
        out_shape=jax.ShapeDtypeStruct((B, L, GDN_W), BF16),
        scratch_shapes=[pltpu.VMEM((GDN_HEADS, GDN_DK, GDN_DV), F32)],
        compiler_params=_cparams(("parallel", "arbitrary")),
        name="gdn_scan",
    )(qn, kn, vn, P, gb, norm_w)


def _rope128(x, cos, sin):
    return x * cos + pltpu.roll(x, ATT_HD // 2, 1) * sin


def _rope64(x, cos, sin, first_half):
    rot = jnp.where(first_half, pltpu.roll(x, LANES - IDX_HD // 2, 1), pltpu.roll(x, IDX_HD // 2, 1))
    return x * cos + rot * sin


def _dsa_prep_kernel(aq_ref, akv_ref, iq_ref, s_ref, ca_ref, sa_ref, ci_ref, si_ref,
                     qa_ref, ka_ref, vt_ref, qi_ref, ki_ref, wt_ref, *, n_real):
    n = pl.program_id(1)
    live = n < n_real
    ca, sa, ci, si = ca_ref[...], sa_ref[...], ci_ref[...], si_ref[...]
    lane = lax.broadcasted_iota(I32, (BLK, LANES), 1)
    first_half = (lane % IDX_HD) < (IDX_HD // 2)
    zero = jnp.zeros((BLK, LANES), F32)

    for h in range(ATT_HEADS):
        sl = slice(h * ATT_HD, (h + 1) * ATT_HD)
        qa_ref[0, :, sl] = _rope128(aq_ref[0, :, sl], ca, sa).astype(BF16)
    for g in range(ATT_KV_HEADS):
        sl = slice(g * ATT_HD, (g + 1) * ATT_HD)
        k = _rope128(akv_ref[0, :, sl], ca, sa)
        ka_ref[0, :, sl] = jnp.where(live, k, zero).astype(BF16)
        v = akv_ref[0, :, ATT_KV_W + g * ATT_HD:ATT_KV_W + (g + 1) * ATT_HD]
        vt_ref[0, 0, g, :ATT_HD, :] = jnp.where(live, v, zero).T.astype(BF16)
        vt_ref[0, 0, g, ATT_HD:, :] = jnp.ones((8, BLK), BF16)
    for p in range(IDX_Q_W // LANES):
        sl = slice(p * LANES, (p + 1) * LANES)
        qi_ref[0, :, sl] = _rope64(iq_ref[0, :, sl], ci, si, first_half).astype(BF16)
    s = s_ref[0]
    ki = _rope64(s, ci, si, first_half)
    ki_ref[0] = jnp.where(live, ki, zero)[:, :IDX_HD].astype(BF16)
    wsc = s * ((IDX_HEADS * IDX_HD) ** -0.5)
    wh, wm, wl = _split3(wsc)
    sel = (lax.broadcasted_iota(I32, (16, LANES), 1)
           == lax.broadcasted_iota(I32, (16, LANES), 0) + S_IW).astype(BF16)
    wt = lax.dot_general(jnp.concatenate([sel, sel, sel], axis=1),
                         jnp.concatenate([wh, wm, wl], axis=1),
                         (((1,), (1,)), ((), ())), preferred_element_type=F32)
    wt_ref[0] = wt[:IDX_HEADS]


def _dsa_prep(P, Ps, cos_a, sin_a, cos_i, sin_i, nkb):
    B, TP, _ = P.shape
    n_real = TP // BLK
    TK = nkb * BLK
    cl = lambda b, n: (b, jnp.minimum(n, n_real - 1), 0)
    tab = lambda b, n: (jnp.minimum(n, n_real - 1), 0)
    return pl.pallas_call(
        functools.partial(_dsa_prep_kernel, n_real=n_real),
        grid=(B, nkb),
        in_specs=[
            pl.BlockSpec((1, BLK, ATT_Q_W), lambda b, n: (b, jnp.minimum(n, n_real - 1), C_AQ // ATT_Q_W)),
            pl.BlockSpec((1, BLK, 2 * ATT_KV_W), lambda b, n: (b, jnp.minimum(n, n_real - 1), C_AK // (2 * ATT_KV_W))),
            pl.BlockSpec((1, BLK, IDX_Q_W), lambda b, n: (b, jnp.minimum(n, n_real - 1), C_IQ // IDX_Q_W)),
            pl.BlockSpec((1, BLK, LANES), cl),
            pl.BlockSpec((BLK, LANES), tab),
            pl.BlockSpec((BLK, LANES), tab),
            pl.BlockSpec((BLK, LANES), tab),
            pl.BlockSpec((BLK, LANES), tab),
        ],
        out_specs=[
            pl.BlockSpec((1, BLK, ATT_Q_W), lambda b, n: (b, n, 0)),
            pl.BlockSpec((1, BLK, ATT_KV_W), lambda b, n: (b, n, 0)),
            pl.BlockSpec((1, 1, ATT_KV_HEADS, ATT_HD + 8, BLK), lambda b, n: (b, n // 2, 0, 0, n % 2)),
            pl.BlockSpec((1, BLK, IDX_Q_W), lambda b, n: (b, n, 0)),
            pl.BlockSpec((1, BLK, IDX_HD), lambda b, n: (b, n, 0)),
            pl.BlockSpec((1, IDX_HEADS, BLK), lambda b, n: (b, 0, n)),
        ],
        out_shape=[
            jax.ShapeDtypeStruct((B, TK, ATT_Q_W), BF16),
            jax.ShapeDtypeStruct((B, TK, ATT_KV_W), BF16),
            jax.ShapeDtypeStruct((B, TK // KEY_TILE, ATT_KV_HEADS, ATT_HD + 8, KEY_TILE), BF16),
            jax.ShapeDtypeStruct((B, TK, IDX_Q_W), BF16),
            jax.ShapeDtypeStruct((B, TK, IDX_HD), BF16),
            jax.ShapeDtypeStruct((B, IDX_HEADS, TK), F32),
        ],
        compiler_params=_cparams(("parallel", "arbitrary")),
        name="dsa_prep",
    )(P, P, P, Ps, cos_a, sin_a, cos_i, sin_i)


def _colsum(x):
    return jnp.sum(x, axis=0, keepdims=True, dtype=x.dtype)


def _fold8(x, op):
    return op(x.reshape(KEY_TILE // 8, 8, BLK), axis=0)


def _dsa_kernel(qa_ref, qi_ref, wt_ref, ka_ref, vt_ref, ki_ref, o_ref,
                s_ref, bias_ref, *, topk, softmax_scale):
    i = pl.program_id(1)
    q_row0 = (i + 1) * BLK
    nt = (i + 3) // 2
    kt = KEY_TILE
    qrow = lax.broadcasted_iota(I32, (kt, BLK), 1) + q_row0

    qi = qi_ref[0]
    qih = [qi[:, h * IDX_HD:(h + 1) * IDX_HD] for h in range(IDX_HEADS)]
    wt = wt_ref[0]

    def score_tile(t, carry):
        lo8, hi8 = carry
        r0 = pl.multiple_of(t * kt, kt)
        kblk = ki_ref[0, pl.ds(r0, kt), :]
        s = jnp.zeros((kt, BLK), F32)
        for h in range(IDX_HEADS):
            d = lax.dot_general(kblk, qih[h], (((1,), (1,)), ((), ())), preferred_element_type=F32)
            s = s + jnp.maximum(d, 0.0) * wt[h:h + 1, :]
        krow = lax.broadcasted_iota(I32, (kt, BLK), 0) + r0
        valid = (krow >= PAD_ROWS) & (krow <= qrow)
        s_ref[pl.ds(r0, kt), :] = jnp.where(valid, s, -jnp.inf)
        return jnp.minimum(lo8, _fold8(s, jnp.min)), jnp.maximum(hi8, _fold8(s, jnp.max))

    lo8, hi8 = lax.fori_loop(0, nt, score_tile,
                             (jnp.full((8, BLK), jnp.inf, F32), jnp.full((8, BLK), -jnp.inf, F32)))

    def count_pass(p):
        def body(t, acc):
            r0 = pl.multiple_of(t * kt, kt)
            ge = s_ref[pl.ds(r0, kt), :] >= p
            return acc + _fold8(jnp.where(ge, 1, 0).astype(I32), functools.partial(jnp.sum, dtype=I32))
        return _colsum(lax.fori_loop(0, nt, body, jnp.zeros((8, BLK), I32)))

    def snap_pass(p):
        def body(t, carry):
            acc, up, dn = carry
            r0 = pl.multiple_of(t * kt, kt)
            s = s_ref[pl.ds(r0, kt), :]
            ge = s >= p
            acc = acc + _fold8(jnp.where(ge, 1, 0).astype(I32), functools.partial(jnp.sum, dtype=I32))
            up = jnp.minimum(up, _fold8(jnp.where(ge, s, jnp.inf), jnp.min))
            dn = jnp.maximum(dn, _fold8(jnp.where(ge, -jnp.inf, s), jnp.max))
            return acc, up, dn
        acc, up, dn = lax.fori_loop(0, nt, body, (jnp.zeros((8, BLK), I32), jnp.full((8, BLK), jnp.inf, F32),
                                                  jnp.full((8, BLK), -jnp.inf, F32)))
        return _colsum(acc), jnp.min(up, axis=0, keepdims=True), jnp.max(dn, axis=0, keepdims=True)

    def n_active(clo, lo, hi):
        return jnp.max(jnp.where((clo > topk) & (lo < hi), 1, 0).astype(I32))

    lo = jnp.min(lo8, axis=0, keepdims=True)
    hi = jnp.max(hi8, axis=0, keepdims=True)
    clo = lax.broadcasted_iota(I32, (1, BLK), 1) + (q_row0 - PAD_ROWS + 1)
    chi = jnp.zeros((1, BLK), I32)

    def fast_cond(c):
        it, lo, hi, clo, chi = c
        return (it < FAST_PROBES) & (n_active(clo, lo, hi) > 0)

    def fast_body(c):
        it, lo, hi, clo, chi = c
        p = 0.5 * lo + 0.5 * hi
        cnt = count_pass(p)
        act = (clo > topk) & (p > lo) & (p < hi)
        up = act & (cnt >= topk)
        dn = act & (cnt < topk)
        return (it + 1, jnp.where(up, p, lo), jnp.where(dn, p, hi), jnp.where(up, cnt, clo), jnp.where(dn, cnt, chi))

    _, lo, hi, clo, chi = lax.while_loop(fast_cond, fast_body, (jnp.int32(0), lo, hi, clo, chi))

    def slow_cond(c):
        lo, hi, clo, chi = c
        return n_active(clo, lo, hi) > 0

    def slow_body(c):
        lo, hi, clo, chi = c
        mid = 0.5 * lo + 0.5 * hi
        p = jnp.where(mid > lo, jnp.minimum(mid, hi), hi)
        cnt, nxt, prv = snap_pass(p)
        act = (clo > topk) & (lo < hi)
        up = act & (cnt >= topk)
        dn = act & (cnt < topk)
        return (jnp.where(up, nxt, lo), jnp.where(dn, prv, hi), jnp.where(up, cnt, clo), jnp.where(dn, cnt, chi))

    lo, hi, clo, chi = lax.while_loop(slow_cond, slow_body, (lo, hi, clo, chi))
    thr = lo
    need = (topk - chi).astype(F32)
    any_tie = jnp.max(jnp.where(clo > topk, 1, 0).astype(I32))

    def plain_bias(t, carry):
        r0 = pl.multiple_of(t * kt, kt)
        bias_ref[pl.ds(r0, kt), :] = jnp.where(s_ref[pl.ds(r0, kt), :] >= thr, 0.0, NEG_BIG)
        return carry

    def tie_bias(t, run):
        r0 = pl.multiple_of(t * kt, kt)
        s = s_ref[pl.ds(r0, kt), :]
        eq = (s == thr) & (clo > topk)
        lower = (lax.broadcasted_iota(I32, (kt, kt), 0) >= lax.broadcasted_iota(I32, (kt, kt), 1)).astype(BF16)
        e = jnp.where(eq, 1.0, 0.0)
        rank = jnp.dot(lower, e.astype(BF16), preferred_element_type=F32) + run
        keep = (eq & (rank <= need)) | (jnp.logical_not(eq) & (s >= thr))
        bias_ref[pl.ds(r0, kt), :] = jnp.where(keep, 0.0, NEG_BIG)
        return run + jnp.sum(e, axis=0, keepdims=True)

    @pl.when(any_tie == 0)
    def _():
        lax.fori_loop(0, nt, plain_bias, 0)

    @pl.when(any_tie != 0)
    def _():
        lax.fori_loop(0, nt, tie_bias, jnp.zeros((1, BLK), F32))

    c_exp = softmax_scale * math.log2(math.e)
    for g in range(ATT_KV_HEADS):
        gw = ATT_GROUP * BLK
        qg = jnp.concatenate([qa_ref[0, :, (g * ATT_GROUP + r) * ATT_HD:(g * ATT_GROUP + r + 1) * ATT_HD]
                              for r in range(ATT_GROUP)], axis=0)

        def att_tile(t, carry, g=g, qg=qg):
            m, acc = carry
            r0 = pl.multiple_of(t * kt, kt)
            kblk = ka_ref[0, pl.ds(r0, kt), g * ATT_HD:(g + 1) * ATT_HD]
            x = lax.dot_general(kblk, qg, (((1,), (1,)), ((), ())), preferred_element_type=F32)
            b = bias_ref[pl.ds(r0, kt), :]
            x = x + jnp.concatenate([b] * ATT_GROUP, axis=1)
            m_new = jnp.maximum(m, jnp.max(x, axis=0, keepdims=True))
            alpha = jnp.exp2((m - m_new) * c_exp)
            p = jnp.exp2((x - m_new) * c_exp)
            acc = acc * alpha + jnp.dot(vt_ref[0, t, g], p.astype(BF16), preferred_element_type=F32)
            return m_new, acc

        m0 = jnp.full((1, gw), NEG_BIG, F32)
        a0 = jnp.zeros((ATT_HD + 8, gw), F32)
        _, acc = lax.fori_loop(0, nt, att_tile, (m0, a0))
        out_t = acc[:ATT_HD] / acc[ATT_HD:ATT_HD + 1]
        for r in range(ATT_GROUP):
            hh = g * ATT_GROUP + r
            o_ref[0, :, hh * ATT_HD:(hh + 1) * ATT_HD] = out_t[:, r * BLK:(r + 1) * BLK].T.astype(o_ref.dtype)


def _dsa(qa, qi, wt, ka, vt, ki, L, topk):
    B, TK, _ = qa.shape
    nq = L // BLK
    return pl.pallas_call(
        functools.partial(_dsa_kernel, topk=topk, softmax_scale=ATT_HD ** -0.5),
        grid=(B, nq),
        in_specs=[
            pl.BlockSpec((1, BLK, ATT_Q_W), lambda b, i: (b, i + 1, 0)),
            pl.BlockSpec((1, BLK, IDX_Q_W), lambda b, i: (b, i + 1, 0)),
            pl.BlockSpec((1, IDX_HEADS, BLK), lambda b, i: (b, 0, i + 1)),
            pl.BlockSpec((1, TK, ATT_KV_W), lambda b, i: (b, 0, 0)),
            pl.BlockSpec((1, TK // KEY_TILE, ATT_KV_HEADS, ATT_HD + 8, KEY_TILE), lambda b, i: (b, 0, 0, 0, 0)),
            pl.BlockSpec((1, TK, IDX_HD), lambda b, i: (b, 0, 0)),
        ],
        out_specs=pl.BlockSpec((1, BLK, ATT_Q_W), lambda b, i: (b, i, 0)),
        out_shape=jax.ShapeDtypeStruct((B, L, ATT_Q_W), BF16),
        scratch_shapes=[pltpu.VMEM((TK, BLK), F32), pltpu.VMEM((TK, BLK), F32)],
        compiler_params=_cparams(("parallel", "arbitrary")),
        name="dsa_attn",
    )(qa, qi, wt, ka, vt, ki)


def _merge_kernel(x_ref, yg_ref, yd_ref, ga_ref, gb_ref, wg_ref, wd_ref, wo_ref,
                  npost_ref, npre_ref, h_ref, n2_ref):
    bg = jnp.dot(yg_ref[...], wg_ref[...], preferred_element_type=F32)
    bd = jnp.dot(yd_ref[...], wd_ref[...], preferred_element_type=F32)
    merged = jax.nn.sigmoid(ga_ref[...]) * bg + jax.nn.sigmoid(gb_ref[...]) * bd
    mix = jnp.dot(merged.astype(BF16), wo_ref[...], preferred_element_type=F32)
    h1 = x_ref[...] + _rms(mix, npost_ref[...])
    h_ref[...] = h1
    n2_ref[...] = _rms(h1, npre_ref[...]).astype(BF16)


def _merge(x2, yg, yd, P2, wg, wd, wo, npost, npre, L, TP, tm):
    R = x2.shape[0]
    per = L // tm
    assert tm == BLK, "gate tiles are addressed in BLK-row blocks"
    gate_row = lambda i: (i // per) * (TP // BLK) + 1 + (i % per)
    row = lambda i: (i, 0)
    full = lambda i: (0, 0)
    return pl.pallas_call(
        _merge_kernel,
        grid=(R // tm,),
        in_specs=[
            pl.BlockSpec((tm, D_MODEL), row),
            pl.BlockSpec((tm, GDN_W), row),
            pl.BlockSpec((tm, ATT_Q_W), row),
            pl.BlockSpec((tm, D_MODEL), lambda i: (gate_row(i), C_GA // D_MODEL)),
            pl.BlockSpec((tm, D_MODEL), lambda i: (gate_row(i), C_GB // D_MODEL)),
            pl.BlockSpec((GDN_W, D_MODEL), full),
            pl.BlockSpec((ATT_Q_W, D_MODEL), full),
            pl.BlockSpec((D_MODEL, D_MODEL), full),
            pl.BlockSpec((1, D_MODEL), full),
            pl.BlockSpec((1, D_MODEL), full),
        ],
        out_specs=[pl.BlockSpec((tm, D_MODEL), row), pl.BlockSpec((tm, D_MODEL), row)],
        out_shape=[jax.ShapeDtypeStruct((R, D_MODEL), F32), jax.ShapeDtypeStruct((R, D_MODEL), BF16)],
        compiler_params=_cparams(("parallel",)),
        name="merge_out",
    )(x2, yg, yd, P2, P2, wg, wd, wo, npost, npre)


def _mlp_kernel(n_ref, h_ref, wu_ref, wd_ref, g_ref, o_ref, acc_ref):
    j = pl.program_id(1)

    @pl.when(j == 0)
    def _():
        acc_ref[...] = jnp.zeros_like(acc_ref)

    u = jnp.maximum(jnp.dot(n_ref[...], wu_ref[...], preferred_element_type=F32), 0.0)
    acc_ref[...] += jnp.dot((u * u).astype(BF16), wd_ref[...], preferred_element_type=F32)

    @pl.when(j == pl.num_programs(1) - 1)
    def _():
        o_ref[...] = h_ref[...] + _rms(acc_ref[...], g_ref[...])


def _mlp(n2, h1, wu, wd, g, tm, tf):
    R = n2.shape[0]
    return pl.pallas_call(
        _mlp_kernel,
        grid=(R // tm, D_FF // tf),
        in_specs=[
            pl.BlockSpec((tm, D_MODEL), lambda i, j: (i, 0)),
            pl.BlockSpec((tm, D_MODEL), lambda i, j: (i, 0)),
            pl.BlockSpec((D_MODEL, tf), lambda i, j: (0, j)),
            pl.BlockSpec((tf, D_MODEL), lambda i, j: (j, 0)),
            pl.BlockSpec((1, D_MODEL), lambda i, j: (0, 0)),
        ],
        out_specs=pl.BlockSpec((tm, D_MODEL), lambda i, j: (i, 0)),
        out_shape=jax.ShapeDtypeStruct((R, D_MODEL), F32),
        scratch_shapes=[pltpu.VMEM((tm, D_MODEL), F32)],
        compiler_params=_cparams(("parallel", "arbitrary")),
        name="mlp",
    )(n2, h1, wu, wd, g)


def _largest_tile(n, candidates):
    for c in candidates:
        if n % c == 0:
            return c
    raise ValueError(f"no tile for {n}")


def _rope_tables(tp, tk):
    pos = jnp.maximum(jnp.arange(tk, dtype=jnp.int32) - PAD_ROWS, 0)
    pos = jnp.where(jnp.arange(tk) < tp, pos, 0).astype(F32)

    def table(dim):
        inv = ROPE_THETA ** (-jnp.arange(0, dim, 2, dtype=F32) / dim)
        ang = pos[:, None] * inv[None, :]
        c, s = jnp.cos(ang), jnp.sin(ang)
        reps = LANES // dim
        return jnp.tile(jnp.concatenate([c, c], axis=1), (1, reps)), jnp.tile(jnp.concatenate([-s, s], axis=1), (1, reps))

    return table(ATT_HD) + table(IDX_HD)


def kernel(x, meta_tokens, pre_mix_norm, w_in, conv_w, a_log, dt_bias, gdn_norm, w_branch_gdn,
           w_branch_dsa, w_out, post_mix_norm, pre_mlp_norm, w_up, w_down, post_mlp_norm):
    B, L, D = x.shape
    assert D == D_MODEL and L % BLK == 0 and w_in.shape[0] == 1, "single layer, 128-aligned sequence"
    TP = BLK + L
    nkb = -(-(TP // BLK) // 2) * 2
    topk = min(TOPK_MAX, L // 4)

    w = w_in[0]
    o = np.cumsum((0, GDN_W, GDN_W, GDN_W, GDN_W, GDN_HEADS, GDN_HEADS, ATT_Q_W, ATT_KV_W, ATT_KV_W,
                   IDX_Q_W, IDX_HD, IDX_HEADS, D_MODEL, D_MODEL))
    seg = lambda i: w[:, o[i]:o[i + 1]]
    w_main = jnp.concatenate([seg(0), seg(1), seg(2), seg(3), seg(6), seg(7), seg(8), seg(9), seg(12), seg(13)],
                             axis=1).astype(BF16)
    w_small = jnp.concatenate([seg(10), seg(11), seg(4), seg(5),
                               jnp.zeros((D, LANES - IDX_HD - IDX_HEADS - 2 * GDN_HEADS), F32)], axis=1)
    lane_vec = lambda v, off: jnp.zeros((1, LANES), F32).at[0, off:off + v.shape[0]].set(v.astype(F32))
    alog_v = lane_vec(a_log[0], S_DECAY)
    dtb_v = lane_vec(dt_bias[0], S_DECAY)

    hp = jnp.concatenate([jnp.zeros((B, PAD_ROWS, D), x.dtype),
                          jnp.broadcast_to(meta_tokens.astype(x.dtype)[None], (B, N_META, D)), x], axis=1)
    R = B * TP
    tm = _largest_tile(R, (1024, 768, 512, 384, 256, 128))
    P2, Ps2 = _proj(hp.reshape(R, D), pre_mix_norm[0][None], w_main, w_small, tm, 1024)
    P = P2.reshape(B, TP, N_MAIN)
    Ps = Ps2.reshape(B, TP, LANES)

    qn, kn, vn, gb = _gdn_prep(P, Ps, conv_w[0], alog_v, dtb_v)
    y_gdn = _gdn(qn, kn, vn, P, gb, gdn_norm[0][None])

    cos_a, sin_a, cos_i, sin_i = _rope_tables(TP, nkb * BLK)
    qa, ka, vt, qi, ki, wt = _dsa_prep(P, Ps, cos_a, sin_a, cos_i, sin_i, nkb)
    y_dsa = _dsa(qa, qi, wt, ka, vt, ki, L, topk)

    Rr = B * L
    h1, n2 = _merge(x.reshape(Rr, D), y_gdn.reshape(Rr, GDN_W), y_dsa.reshape(Rr, ATT_Q_W), P2,
                    w_branch_gdn[0].astype(BF16), w_branch_dsa[0].astype(BF16), w_out[0].astype(BF16),
                    post_mix_norm[0][None], pre_mlp_norm[0][None], L, TP, BLK)
    tmm = _largest_tile(Rr, (1024, 512, 256, 128))
    out = _mlp(n2, h1, w_up[0].astype(BF16), w_down[0].astype(BF16), post_mlp_norm[0][None], tmm, 512)
    return out.reshape(B, L, D)
```

```python
import functools
import math

import jax
import jax.numpy as jnp
import numpy as np
from jax import lax
from jax.experimental import pallas as pl
from jax.experimental.pallas import tpu as pltpu

F32 = jnp.float32
BF16 = jnp.bfloat16
I32 = jnp.int32

D_MODEL = 1024
N_META = 16
GDN_HEADS = 8
GDN_DK = 128
GDN_DV = 128
CONV_WIDTH = 4
ATT_HEADS = 8
ATT_KV_HEADS = 2
ATT_HD = 128
ATT_GROUP = ATT_HEADS // ATT_KV_HEADS
IDX_HEADS = 8
IDX_HD = 64
TOPK_MAX = 256
ROPE_THETA = 10000.0
D_FF = 4 * D_MODEL
EPS = 1e-6

GDN_W = GDN_HEADS * GDN_DK
ATT_Q_W = ATT_HEADS * ATT_HD
ATT_KV_W = ATT_KV_HEADS * ATT_HD
IDX_Q_W = IDX_HEADS * IDX_HD

LANES = 128
SUBLANES = 8
BLK = 128
PAD_ROWS = BLK - N_META
KEY_TILE = 256
WIDE_TILE = 512
VMEM_LIMIT = 56 * 1024 * 1024
PROJ_ROW_TILES = (1024, 768, 512, 384, 256, 128)
PROJ_COL_TILE = 1024
MLP_ROW_TILES = (1024, 512, 256, 128)
MLP_FF_TILE = 512

C_GQ, C_GK, C_GV, C_GZ = 0, 1024, 2048, 3072
C_AQ, C_AK, C_AV, C_IQ = 4096, 5120, 5376, 5632
C_GA, C_GB = 6144, 7168
N_MAIN = 8192
S_IK, S_IW, S_BETA, S_DECAY = 0, 64, 72, 80

NEG_BIG = -1e30
PROBES_PER_CHECK = 4
FAST_PROBES = 28


def _cparams(sem):
    return pltpu.CompilerParams(dimension_semantics=sem, vmem_limit_bytes=VMEM_LIMIT)


def _split3(a):
    h = a.astype(BF16)
    r = a - h.astype(F32)
    m = r.astype(BF16)
    l = (r - m.astype(F32)).astype(BF16)
    return h, m, l


def _split2(a):
    h = a.astype(BF16)
    return h, (a - h.astype(F32)).astype(BF16)


def _rms(x, g):
    return x * lax.rsqrt(jnp.mean(x * x, axis=-1, keepdims=True) + EPS) * g


def _nt_dot(a, b):
    return lax.dot_general(a, b, (((1,), (1,)), ((), ())), preferred_element_type=F32)


def _proj_kernel(h_ref, g_ref, w_ref, ws_ref, o_ref, os_ref, n_ref):
    j = pl.program_id(1)

    @pl.when(j == 0)
    def _():
        n = _rms(h_ref[...], g_ref[...])
        n_ref[...] = n.astype(BF16)
        nh, nl = _split2(n)
        wh, wl = _split2(ws_ref[...])
        os_ref[...] = (jnp.dot(nh, wh, preferred_element_type=F32)
                       + jnp.dot(nl, wh, preferred_element_type=F32)
                       + jnp.dot(nh, wl, preferred_element_type=F32))

    o_ref[...] = jnp.dot(n_ref[...], w_ref[...], preferred_element_type=F32)


def _proj(hp, g, w_main, w_small, tm, tn):
    R = hp.shape[0]
    return pl.pallas_call(
        _proj_kernel,
        grid=(R // tm, N_MAIN // tn),
        in_specs=[
            pl.BlockSpec((tm, D_MODEL), lambda i, j: (i, 0)),
            pl.BlockSpec((1, D_MODEL), lambda i, j: (0, 0)),
            pl.BlockSpec((D_MODEL, tn), lambda i, j: (0, j)),
            pl.BlockSpec((D_MODEL, LANES), lambda i, j: (0, 0)),
        ],
        out_specs=[
            pl.BlockSpec((tm, tn), lambda i, j: (i, j)),
            pl.BlockSpec((tm, LANES), lambda i, j: (i, 0)),
        ],
        out_shape=[
            jax.ShapeDtypeStruct((R, N_MAIN), F32),
            jax.ShapeDtypeStruct((R, LANES), F32),
        ],
        scratch_shapes=[pltpu.VMEM((tm, D_MODEL), BF16)],
        compiler_params=_cparams(("parallel", "arbitrary")),
        name="in_proj",
    )(hp, g, w_main, w_small)


def _mm3(a_parts, b_parts):
    ah, al = a_parts
    bh, bl = b_parts
    lhs = jnp.concatenate([ah, al, ah], axis=1)
    rhs = jnp.concatenate([bh, bh, bl], axis=0)
    return jnp.dot(lhs, rhs, preferred_element_type=F32)


def _unit_lower_inverses(a_list, eye):
    heads = range(len(a_list))
    sp = [_split2(a) for a in a_list]
    p = [_mm3(sp[h], sp[h]) for h in heads]
    x = [eye - a for a in a_list]
    for _ in range(5):
        xs = [_split2(v) for v in x]
        ps = [_split2(v) for v in p]
        r = [_mm3((jnp.concatenate([xs[h][0], ps[h][0]], axis=0), jnp.concatenate([xs[h][1], ps[h][1]], axis=0)),
                  ps[h]) for h in heads]
        x = [x[h] + r[h][:BLK] for h in heads]
        p = [r[h][BLK:] for h in heads]
    xs = [_split2(v) for v in x]
    ps = [_split2(v) for v in p]
    return [x[h] + _mm3(xs[h], ps[h]) for h in heads]


def _gdn_kernel(x_ref, xp_ref, ps_ref, z_ref, cw_ref, alog_ref, dtb_ref, nw_ref, o_ref, s_ref, buf_ref):
    n = pl.program_id(1)
    heads = range(GDN_HEADS)

    @pl.when(n == 0)
    def _():
        s_ref[...] = jnp.zeros_like(s_ref)

    prev = xp_ref[0]
    buf_ref[0:SUBLANES, :] = jnp.where(n == 0, jnp.zeros_like(prev), prev)
    buf_ref[SUBLANES:SUBLANES + BLK, :] = x_ref[0]
    first = SUBLANES - (CONV_WIDTH - 1)
    y = buf_ref[pl.ds(first, BLK), :] * cw_ref[0:1, :]
    for t in range(1, CONV_WIDTH):
        y = y + buf_ref[pl.ds(first + t, BLK), :] * cw_ref[t:t + 1, :]
    y = y * jax.nn.sigmoid(y)

    s = ps_ref[0]
    row = lax.broadcasted_iota(I32, (BLK, LANES), 0) + n * BLK
    lane = lax.broadcasted_iota(I32, (BLK, LANES), 1)
    xs = s + dtb_ref[...]
    softplus = jnp.maximum(xs, 0.0) + jnp.log(1.0 + jnp.exp(-jnp.abs(xs)))
    gb = jnp.where((lane >= S_DECAY) & (lane < S_DECAY + GDN_HEADS), -jnp.exp(alog_ref[...]) * softplus,
                   jnp.where((lane >= S_BETA) & (lane < S_BETA + GDN_HEADS), jax.nn.sigmoid(s), 0.0))
    gb = jnp.where(row >= PAD_ROWS, gb, 0.0)

    ri = lax.broadcasted_iota(I32, (BLK, BLK), 0)
    ci = lax.broadcasted_iota(I32, (BLK, BLK), 1)
    tri_incl = ri >= ci
    tri_strict = ri > ci
    eye = (ri == ci).astype(F32)
    ones_l = tri_incl.astype(BF16)

    gh, gm, gl = _split3(gb)
    gc = jnp.dot(jnp.concatenate([ones_l, ones_l, ones_l], axis=1),
                 jnp.concatenate([gh, gm, gl], axis=0), preferred_element_type=F32)
    sel = (lax.broadcasted_iota(I32, (16, LANES), 1)
           == lax.broadcasted_iota(I32, (16, LANES), 0) + S_DECAY).astype(BF16)
    ch, cm, cl = _split3(gc)
    gct = _nt_dot(jnp.concatenate([sel, sel, sel], axis=1), jnp.concatenate([ch, cm, cl], axis=1))

    def head_cols(base, h):
        return y[:, base + h * GDN_DK:base + (h + 1) * GDN_DK]

    q = [head_cols(C_GQ, h) for h in heads]
    k = [head_cols(C_GK, h) for h in heads]
    v = [head_cols(C_GV, h) for h in heads]
    q = [a * lax.rsqrt(jnp.sum(a * a, axis=-1, keepdims=True) + EPS) * (GDN_DK ** -0.5) for a in q]
    k = [a * lax.rsqrt(jnp.sum(a * a, axis=-1, keepdims=True) + EPS) for a in k]
    beta = [gb[:, S_BETA + h:S_BETA + h + 1] for h in heads]
    gcc = [gc[:, S_DECAY + h:S_DECAY + h + 1] for h in heads]
    g_last = [c[BLK - 1:BLK, :] for c in gcc]
    decay = [jnp.where(tri_incl, jnp.exp(jnp.minimum(gcc[h] - gct[h:h + 1, :], 0.0)), 0.0) for h in heads]
    e_gc = [jnp.exp(c) for c in gcc]
    kb = [k[h] * beta[h] for h in heads]
    kbf = [a.astype(BF16) for a in k]
    kk = [_nt_dot(kb[h].astype(BF16), kbf[h]) for h in heads]
    qk = [_nt_dot(q[h].astype(BF16), kbf[h]) for h in heads]
    tm = _unit_lower_inverses([jnp.where(tri_strict, kk[h] * decay[h], 0.0) for h in heads], eye)
    uw = [jnp.dot(tm[h].astype(BF16), jnp.concatenate([v[h] * beta[h], kb[h] * e_gc[h]], axis=1).astype(BF16),
                  preferred_element_type=F32) for h in heads]
    attn = [(qk[h] * decay[h]).astype(BF16) for h in heads]
    q_dec = [(q[h] * e_gc[h]).astype(BF16) for h in heads]
    k_tail = [(k[h] * jnp.exp(g_last[h] - gcc[h])).T.astype(BF16) for h in heads]

    s_old = [s_ref[h] for h in heads]
    s_bf = [a.astype(BF16) for a in s_old]
    v_new = [uw[h][:, :GDN_DV] - jnp.dot(uw[h][:, GDN_DV:].astype(BF16), s_bf[h], preferred_element_type=F32)
             for h in heads]
    vn_bf = [a.astype(BF16) for a in v_new]
    o = [jnp.dot(q_dec[h], s_bf[h], preferred_element_type=F32)
         + jnp.dot(attn[h], vn_bf[h], preferred_element_type=F32) for h in heads]
    for h in heads:
        s_ref[h] = s_old[h] * jnp.exp(g_last[h]) + jnp.dot(k_tail[h], vn_bf[h], preferred_element_type=F32)
    for h in heads:
        z = z_ref[0, :, h * GDN_DV:(h + 1) * GDN_DV]
        o_ref[0, :, h * GDN_DV:(h + 1) * GDN_DV] = (_rms(o[h], nw_ref[...]) * (z * jax.nn.sigmoid(z))).astype(o_ref.dtype)


def _gdn(P, Ps, conv_w, alog_v, dtb_v, norm_w):
    B, TP, _ = P.shape
    nb = TP // BLK
    L = TP - BLK
    wq = 3 * GDN_W
    const = lambda b, n: (0, 0)
    return pl.pallas_call(
        _gdn_kernel,
        grid=(B, nb),
        in_specs=[
            pl.BlockSpec((1, BLK, wq), lambda b, n: (b, n, 0)),
            pl.BlockSpec((1, SUBLANES, wq), lambda b, n: (b, jnp.maximum(n * (BLK // SUBLANES) - 1, 0), 0)),
            pl.BlockSpec((1, BLK, LANES), lambda b, n: (b, n, 0)),
            pl.BlockSpec((1, BLK, GDN_W), lambda b, n: (b, n, C_GZ // GDN_W)),
            pl.BlockSpec((CONV_WIDTH, wq), const),
            pl.BlockSpec((1, LANES), const),
            pl.BlockSpec((1, LANES), const),
            pl.BlockSpec((1, GDN_DV), const),
        ],
        out_specs=pl.BlockSpec((1, BLK, GDN_W), lambda b, n: (b, jnp.maximum(n - 1, 0), 0)),
        out_shape=jax.ShapeDtypeStruct((B, L, GDN_W), BF16),
        scratch_shapes=[pltpu.VMEM((GDN_HEADS, GDN_DK, GDN_DV), F32), pltpu.VMEM((SUBLANES + BLK, wq), F32)],
        compiler_params=_cparams(("parallel", "arbitrary")),
        name="gdn_scan",
    )(P, P, Ps, P, conv_w, alog_v, dtb_v, norm_w)


def _rope128(x, cos, sin):
    return x * cos + pltpu.roll(x, ATT_HD // 2, 1) * sin


def _rope64(x, cos, sin, first_half):
    rot = jnp.where(first_half, pltpu.roll(x, LANES - IDX_HD // 2, 1), pltpu.roll(x, IDX_HD // 2, 1))
    return x * cos + rot * sin


def _dsa_prep_kernel(aq_ref, akv_ref, iq_ref, s_ref, ca_ref, sa_ref, ci_ref, si_ref,
                     qa_ref, ka_ref, vt_ref, qi_ref, ki_ref, wt_ref, *, n_real):
    n = pl.program_id(1)
    live = n < n_real
    ca, sa, ci, si = ca_ref[...], sa_ref[...], ci_ref[...], si_ref[...]
    lane = lax.broadcasted_iota(I32, (BLK, LANES), 1)
    first_half = (lane % IDX_HD) < (IDX_HD // 2)
    zero = jnp.zeros((BLK, LANES), F32)

    for h in range(ATT_HEADS):
        sl = slice(h * ATT_HD, (h + 1) * ATT_HD)
        qa_ref[0, :, sl] = _rope128(aq_ref[0, :, sl], ca, sa).astype(BF16)
    for g in range(ATT_KV_HEADS):
        sl = slice(g * ATT_HD, (g + 1) * ATT_HD)
        k = _rope128(akv_ref[0, :, sl], ca, sa)
        ka_ref[0, :, sl] = jnp.where(live, k, zero).astype(BF16)
        v = akv_ref[0, :, ATT_KV_W + g * ATT_HD:ATT_KV_W + (g + 1) * ATT_HD]
        vt_ref[0, 0, g, :ATT_HD, :] = jnp.where(live, v, zero).T.astype(BF16)
        vt_ref[0, 0, g, ATT_HD:, :] = jnp.ones((SUBLANES, BLK), BF16)
    for p in range(IDX_Q_W // LANES):
        sl = slice(p * LANES, (p + 1) * LANES)
        qi_ref[0, :, sl] = _rope64(iq_ref[0, :, sl], ci, si, first_half).astype(BF16)
    s = s_ref[0]
    ki = _rope64(s, ci, si, first_half)
    ki_ref[0] = jnp.where(live, ki, zero)[:, :IDX_HD].astype(BF16)
    wsc = s * ((IDX_HEADS * IDX_HD) ** -0.5)
    wh, wm, wl = _split3(wsc)
    sel = (lax.broadcasted_iota(I32, (16, LANES), 1)
           == lax.broadcasted_iota(I32, (16, LANES), 0) + S_IW).astype(BF16)
    wt = _nt_dot(jnp.concatenate([sel, sel, sel], axis=1), jnp.concatenate([wh, wm, wl], axis=1))
    wt_ref[0] = wt[:IDX_HEADS]


def _dsa_prep(P, Ps, cos_a, sin_a, cos_i, sin_i, nkb):
    B, TP, _ = P.shape
    n_real = TP // BLK
    TK = nkb * BLK
    cl = lambda b, n: (b, jnp.minimum(n, n_real - 1), 0)
    tab = lambda b, n: (jnp.minimum(n, n_real - 1), 0)
    return pl.pallas_call(
        functools.partial(_dsa_prep_kernel, n_real=n_real),
        grid=(B, nkb),
        in_specs=[
            pl.BlockSpec((1, BLK, ATT_Q_W), lambda b, n: (b, jnp.minimum(n, n_real - 1), C_AQ // ATT_Q_W)),
            pl.BlockSpec((1, BLK, 2 * ATT_KV_W), lambda b, n: (b, jnp.minimum(n, n_real - 1), C_AK // (2 * ATT_KV_W))),
            pl.BlockSpec((1, BLK, IDX_Q_W), lambda b, n: (b, jnp.minimum(n, n_real - 1), C_IQ // IDX_Q_W)),
            pl.BlockSpec((1, BLK, LANES), cl),
            pl.BlockSpec((BLK, LANES), tab),
            pl.BlockSpec((BLK, LANES), tab),
            pl.BlockSpec((BLK, LANES), tab),
            pl.BlockSpec((BLK, LANES), tab),
        ],
        out_specs=[
            pl.BlockSpec((1, BLK, ATT_Q_W), lambda b, n: (b, n, 0)),
            pl.BlockSpec((1, BLK, ATT_KV_W), lambda b, n: (b, n, 0)),
            pl.BlockSpec((1, 1, ATT_KV_HEADS, ATT_HD + SUBLANES, BLK),
                         lambda b, n: (b, n // (WIDE_TILE // BLK), 0, 0, n % (WIDE_TILE // BLK))),
            pl.BlockSpec((1, BLK, IDX_Q_W), lambda b, n: (b, n, 0)),
            pl.BlockSpec((1, BLK, IDX_HD), lambda b, n: (b, n, 0)),
            pl.BlockSpec((1, IDX_HEADS, BLK), lambda b, n: (b, 0, n)),
        ],
        out_shape=[
            jax.ShapeDtypeStruct((B, TK, ATT_Q_W), BF16),
            jax.ShapeDtypeStruct((B, TK, ATT_KV_W), BF16),
            jax.ShapeDtypeStruct((B, TK // WIDE_TILE, ATT_KV_HEADS, ATT_HD + SUBLANES, WIDE_TILE), BF16),
            jax.ShapeDtypeStruct((B, TK, IDX_Q_W), BF16),
            jax.ShapeDtypeStruct((B, TK, IDX_HD), BF16),
            jax.ShapeDtypeStruct((B, IDX_HEADS, TK), F32),
        ],
        compiler_params=_cparams(("parallel", "arbitrary")),
        name="dsa_prep",
    )(P, P, P, Ps, cos_a, sin_a, cos_i, sin_i)


def _colsum(x):
    return jnp.sum(x, axis=0, keepdims=True, dtype=x.dtype)


def _fold8(x, op):
    return op(x.reshape(KEY_TILE // SUBLANES, SUBLANES, BLK), axis=0)


def _dsa_kernel(qa_ref, qi_ref, wt_ref, ka_ref, vt_ref, ki_ref, o_ref,
                s_ref, bias_ref, acc_ref, x_ref, *, topk, softmax_scale):
    i = pl.program_id(1)
    q_row0 = (i + 1) * BLK
    nt = (i + 3) // 2
    kt = KEY_TILE
    qrow = lax.broadcasted_iota(I32, (kt, BLK), 1) + q_row0

    def rows(t):
        return pl.ds(pl.multiple_of(t * kt, kt), kt)

    qi = qi_ref[0]
    qs = jnp.concatenate([qi[:, h * IDX_HD:(h + 1) * IDX_HD] for h in range(IDX_HEADS)], axis=0)
    wt = wt_ref[0]
    nw = (nt + 1) // 2

    def score_tile(u, carry):
        lo8, hi8 = carry
        r0 = pl.multiple_of(u * WIDE_TILE, WIDE_TILE)
        d = _nt_dot(ki_ref[0, pl.ds(r0, WIDE_TILE), :], qs)
        for half in range(WIDE_TILE // kt):
            s = jnp.zeros((kt, BLK), F32)
            for h in range(IDX_HEADS):
                s = s + jnp.maximum(d[half * kt:(half + 1) * kt, h * BLK:(h + 1) * BLK], 0.0) * wt[h:h + 1, :]
            krow = lax.broadcasted_iota(I32, (kt, BLK), 0) + (r0 + half * kt)
            valid = (krow >= PAD_ROWS) & (krow <= qrow)
            s_ref[pl.ds(r0 + half * kt, kt), :] = jnp.where(valid, s, -jnp.inf)
            lo8 = jnp.minimum(lo8, _fold8(s, jnp.min))
            hi8 = jnp.maximum(hi8, _fold8(s, jnp.max))
        return lo8, hi8

    lo8, hi8 = lax.fori_loop(0, nw, score_tile,
                             (jnp.full((SUBLANES, BLK), jnp.inf, F32), jnp.full((SUBLANES, BLK), -jnp.inf, F32)))

    def count_pass(p):
        def body(t, acc):
            ge = s_ref[rows(t), :] >= p
            return acc + _fold8(jnp.where(ge, 1, 0).astype(I32), functools.partial(jnp.sum, dtype=I32))
        return _colsum(lax.fori_loop(0, nt, body, jnp.zeros((SUBLANES, BLK), I32)))

    def snap_pass(p):
        def body(t, carry):
            acc, up, dn = carry
            s = s_ref[rows(t), :]
            ge = s >= p
            acc = acc + _fold8(jnp.where(ge, 1, 0).astype(I32), functools.partial(jnp.sum, dtype=I32))
            up = jnp.minimum(up, _fold8(jnp.where(ge, s, jnp.inf), jnp.min))
            dn = jnp.maximum(dn, _fold8(jnp.where(ge, -jnp.inf, s), jnp.max))
            return acc, up, dn
        acc, up, dn = lax.fori_loop(0, nt, body, (jnp.zeros((SUBLANES, BLK), I32),
                                                  jnp.full((SUBLANES, BLK), jnp.inf, F32),
                                                  jnp.full((SUBLANES, BLK), -jnp.inf, F32)))
        return _colsum(acc), jnp.min(up, axis=0, keepdims=True), jnp.max(dn, axis=0, keepdims=True)

    def n_active(clo, lo, hi):
        return jnp.max(jnp.where((clo > topk) & (lo < hi), 1, 0).astype(I32))

    lo = jnp.min(lo8, axis=0, keepdims=True)
    hi = jnp.max(hi8, axis=0, keepdims=True)
    clo = lax.broadcasted_iota(I32, (1, BLK), 1) + (q_row0 - PAD_ROWS + 1)
    chi = jnp.zeros((1, BLK), I32)

    def fast_cond(c):
        it, lo, hi, clo, chi = c
        return (it < FAST_PROBES) & (n_active(clo, lo, hi) > 0)

    def fast_body(c):
        it, lo, hi, clo, chi = c
        for _ in range(PROBES_PER_CHECK):
            p = 0.5 * lo + 0.5 * hi
            cnt = count_pass(p)
            act = (clo > topk) & (p > lo) & (p < hi)
            up = act & (cnt >= topk)
            dn = act & (cnt < topk)
            lo, hi, clo, chi = jnp.where(up, p, lo), jnp.where(dn, p, hi), jnp.where(up, cnt, clo), jnp.where(dn, cnt, chi)
        return it + PROBES_PER_CHECK, lo, hi, clo, chi

    _, lo, hi, clo, chi = lax.while_loop(fast_cond, fast_body, (jnp.int32(0), lo, hi, clo, chi))

    def slow_cond(c):
        lo, hi, clo, chi = c
        return n_active(clo, lo, hi) > 0

    def slow_body(c):
        lo, hi, clo, chi = c
        mid = 0.5 * lo + 0.5 * hi
        p = jnp.where(mid > lo, jnp.minimum(mid, hi), hi)
        cnt, nxt, prv = snap_pass(p)
        act = (clo > topk) & (lo < hi)
        up = act & (cnt >= topk)
        dn = act & (cnt < topk)
        return (jnp.where(up, nxt, lo), jnp.where(dn, prv, hi), jnp.where(up, cnt, clo), jnp.where(dn, cnt, chi))

    lo, hi, clo, chi = lax.while_loop(slow_cond, slow_body, (lo, hi, clo, chi))
    thr = lo
    need = (topk - chi).astype(F32)
    any_tie = jnp.max(jnp.where(clo > topk, 1, 0).astype(I32))

    def plain_bias(t, carry):
        bias_ref[rows(t), :] = jnp.where(s_ref[rows(t), :] >= thr, 0.0, NEG_BIG)
        return carry

    def tie_bias(t, run):
        s = s_ref[rows(t), :]
        eq = (s == thr) & (clo > topk)
        lower = (lax.broadcasted_iota(I32, (kt, kt), 0) >= lax.broadcasted_iota(I32, (kt, kt), 1)).astype(BF16)
        e = jnp.where(eq, 1.0, 0.0)
        rank = jnp.dot(lower, e.astype(BF16), preferred_element_type=F32) + run
        keep = (eq & (rank <= need)) | (jnp.logical_not(eq) & (s >= thr))
        bias_ref[rows(t), :] = jnp.where(keep, 0.0, NEG_BIG)
        return run + jnp.sum(e, axis=0, keepdims=True)

    @pl.when(any_tie == 0)
    def _():
        lax.fori_loop(0, 2 * nw, plain_bias, 0)

    @pl.when(any_tie != 0)
    def _():
        lax.fori_loop(0, 2 * nw, tie_bias, jnp.zeros((1, BLK), F32))

    c_exp = softmax_scale * math.log2(math.e)
    gw = ATT_GROUP * BLK
    acc_ref[...] = jnp.zeros_like(acc_ref)

    def att_tile(u, m):
        r0 = pl.multiple_of(u * WIDE_TILE, WIDE_TILE)
        b = bias_ref[pl.ds(r0, WIDE_TILE), :]
        b4 = jnp.concatenate([b] * ATT_GROUP, axis=1)
        for g in range(ATT_KV_HEADS):
            qg = jnp.concatenate([qa_ref[0, :, (g * ATT_GROUP + r) * ATT_HD:(g * ATT_GROUP + r + 1) * ATT_HD]
                                  for r in range(ATT_GROUP)], axis=0)
            x_ref[g] = _nt_dot(ka_ref[0, pl.ds(r0, WIDE_TILE), g * ATT_HD:(g + 1) * ATT_HD], qg)
        m_out = []
        for g in range(ATT_KV_HEADS):
            x = x_ref[g] + b4
            m_new = jnp.maximum(m[g], jnp.max(x, axis=0, keepdims=True))
            alpha = jnp.exp2((m[g] - m_new) * c_exp)
            p = jnp.exp2((x - m_new) * c_exp).astype(BF16)
            acc_ref[g] = acc_ref[g] * alpha + jnp.dot(vt_ref[0, u, g], p, preferred_element_type=F32)
            m_out.append(m_new)
        return tuple(m_out)

    lax.fori_loop(0, nw, att_tile, tuple(jnp.full((1, gw), NEG_BIG, F32) for _ in range(ATT_KV_HEADS)))
    for g in range(ATT_KV_HEADS):
        acc = acc_ref[g]
        out_t = acc[:ATT_HD] / acc[ATT_HD:ATT_HD + 1]
        for r in range(ATT_GROUP):
            hh = g * ATT_GROUP + r
            o_ref[0, :, hh * ATT_HD:(hh + 1) * ATT_HD] = out_t[:, r * BLK:(r + 1) * BLK].T.astype(o_ref.dtype)


def _dsa(qa, qi, wt, ka, vt, ki, L, topk):
    B, TK, _ = qa.shape
    nq = L // BLK
    return pl.pallas_call(
        functools.partial(_dsa_kernel, topk=topk, softmax_scale=ATT_HD ** -0.5),
        grid=(B, nq),
        in_specs=[
            pl.BlockSpec((1, BLK, ATT_Q_W), lambda b, i: (b, i + 1, 0)),
            pl.BlockSpec((1, BLK, IDX_Q_W), lambda b, i: (b, i + 1, 0)),
            pl.BlockSpec((1, IDX_HEADS, BLK), lambda b, i: (b, 0, i + 1)),
            pl.BlockSpec((1, TK, ATT_KV_W), lambda b, i: (b, 0, 0)),
            pl.BlockSpec((1, TK // WIDE_TILE, ATT_KV_HEADS, ATT_HD + SUBLANES, WIDE_TILE), lambda b, i: (b, 0, 0, 0, 0)),
            pl.BlockSpec((1, TK, IDX_HD), lambda b, i: (b, 0, 0)),
        ],
        out_specs=pl.BlockSpec((1, BLK, ATT_Q_W), lambda b, i: (b, i, 0)),
        out_shape=jax.ShapeDtypeStruct((B, L, ATT_Q_W), BF16),
        scratch_shapes=[pltpu.VMEM((TK, BLK), F32), pltpu.VMEM((TK, BLK), F32),
                        pltpu.VMEM((ATT_KV_HEADS, ATT_HD + SUBLANES, ATT_GROUP * BLK), F32),
                        pltpu.VMEM((ATT_KV_HEADS, WIDE_TILE, ATT_GROUP * BLK), F32)],
        compiler_params=_cparams(("parallel", "arbitrary")),
        name="dsa_attn",
    )(qa, qi, wt, ka, vt, ki)


def _merge_kernel(x_ref, yg_ref, yd_ref, ga_ref, gb_ref, wg_ref, wd_ref, wo_ref,
                  npost_ref, npre_ref, h_ref, n2_ref):
    bg = jnp.dot(yg_ref[...], wg_ref[...], preferred_element_type=F32)
    bd = jnp.dot(yd_ref[...], wd_ref[...], preferred_element_type=F32)
    merged = jax.nn.sigmoid(ga_ref[...]) * bg + jax.nn.sigmoid(gb_ref[...]) * bd
    mix = jnp.dot(merged.astype(BF16), wo_ref[...], preferred_element_type=F32)
    h1 = x_ref[...] + _rms(mix, npost_ref[...])
    h_ref[...] = h1
    n2_ref[...] = _rms(h1, npre_ref[...]).astype(BF16)


def _merge(x2, yg, yd, P2, wg, wd, wo, npost, npre, L, TP):
    R = x2.shape[0]
    tm = BLK
    per = L // tm
    gate_row = lambda i: (i // per) * (TP // BLK) + 1 + (i % per)
    row = lambda i: (i, 0)
    full = lambda i: (0, 0)
    return pl.pallas_call(
        _merge_kernel,
        grid=(R // tm,),
        in_specs=[
            pl.BlockSpec((tm, D_MODEL), row),
            pl.BlockSpec((tm, GDN_W), row),
            pl.BlockSpec((tm, ATT_Q_W), row),
            pl.BlockSpec((tm, D_MODEL), lambda i: (gate_row(i), C_GA // D_MODEL)),
            pl.BlockSpec((tm, D_MODEL), lambda i: (gate_row(i), C_GB // D_MODEL)),
            pl.BlockSpec((GDN_W, D_MODEL), full),
            pl.BlockSpec((ATT_Q_W, D_MODEL), full),
            pl.BlockSpec((D_MODEL, D_MODEL), full),
            pl.BlockSpec((1, D_MODEL), full),
            pl.BlockSpec((1, D_MODEL), full),
        ],
        out_specs=[pl.BlockSpec((tm, D_MODEL), row), pl.BlockSpec((tm, D_MODEL), row)],
        out_shape=[jax.ShapeDtypeStruct((R, D_MODEL), F32), jax.ShapeDtypeStruct((R, D_MODEL), BF16)],
        compiler_params=_cparams(("parallel",)),
        name="merge_out",
    )(x2, yg, yd, P2, P2, wg, wd, wo, npost, npre)


def _mlp_kernel(n_ref, h_ref, wu_ref, wd_ref, g_ref, o_ref, acc_ref):
    j = pl.program_id(1)

    @pl.when(j == 0)
    def _():
        acc_ref[...] = jnp.zeros_like(acc_ref)

    u = jnp.maximum(jnp.dot(n_ref[...], wu_ref[...], preferred_element_type=F32), 0.0)
    acc_ref[...] += jnp.dot((u * u).astype(BF16), wd_ref[...], preferred_element_type=F32)

    @pl.when(j == pl.num_programs(1) - 1)
    def _():
        o_ref[...] = h_ref[...] + _rms(acc_ref[...], g_ref[...])


def _mlp(n2, h1, wu, wd, g, tm, tf):
    R = n2.shape[0]
    return pl.pallas_call(
        _mlp_kernel,
        grid=(R // tm, D_FF // tf),
        in_specs=[
            pl.BlockSpec((tm, D_MODEL), lambda i, j: (i, 0)),
            pl.BlockSpec((tm, D_MODEL), lambda i, j: (i, 0)),
            pl.BlockSpec((D_MODEL, tf), lambda i, j: (0, j)),
            pl.BlockSpec((tf, D_MODEL), lambda i, j: (j, 0)),
            pl.BlockSpec((1, D_MODEL), lambda i, j: (0, 0)),
        ],
        out_specs=pl.BlockSpec((tm, D_MODEL), lambda i, j: (i, 0)),
        out_shape=jax.ShapeDtypeStruct((R, D_MODEL), F32),
        scratch_shapes=[pltpu.VMEM((tm, D_MODEL), F32)],
        compiler_params=_cparams(("parallel", "arbitrary")),
        name="mlp",
    )(n2, h1, wu, wd, g)


def _largest_tile(n, candidates):
    for c in candidates:
        if n % c == 0:
            return c
    raise ValueError(f"no tile for {n}")


def _rope_tables(tp, tk):
    pos = jnp.maximum(jnp.arange(tk, dtype=jnp.int32) - PAD_ROWS, 0)
    pos = jnp.where(jnp.arange(tk) < tp, pos, 0).astype(F32)

    def table(dim):
        inv = ROPE_THETA ** (-jnp.arange(0, dim, 2, dtype=F32) / dim)
        ang = pos[:, None] * inv[None, :]
        c, s = jnp.cos(ang), jnp.sin(ang)
        reps = LANES // dim
        return jnp.tile(jnp.concatenate([c, c], axis=1), (1, reps)), jnp.tile(jnp.concatenate([-s, s], axis=1), (1, reps))

    return table(ATT_HD) + table(IDX_HD)


def kernel(x, meta_tokens, pre_mix_norm, w_in, conv_w, a_log, dt_bias, gdn_norm, w_branch_gdn,
           w_branch_dsa, w_out, post_mix_norm, pre_mlp_norm, w_up, w_down, post_mlp_norm):
    B, L, D = x.shape
    assert D == D_MODEL and L % BLK == 0 and w_in.shape[0] == 1, "single layer, 128-aligned sequence"
    TP = BLK + L
    per_wide = WIDE_TILE // BLK
    nkb = -(-(TP // BLK) // per_wide) * per_wide
    topk = min(TOPK_MAX, L // 4)

    w = w_in[0]
    o = np.cumsum((0, GDN_W, GDN_W, GDN_W, GDN_W, GDN_HEADS, GDN_HEADS, ATT_Q_W, ATT_KV_W, ATT_KV_W,
                   IDX_Q_W, IDX_HD, IDX_HEADS, D_MODEL, D_MODEL))
    seg = lambda i: w[:, o[i]:o[i + 1]]
    w_main = jnp.concatenate([seg(0), seg(1), seg(2), seg(3), seg(6), seg(7), seg(8), seg(9), seg(12), seg(13)],
                             axis=1).astype(BF16)
    w_small = jnp.concatenate([seg(10), seg(11), seg(4), seg(5),
                               jnp.zeros((D, LANES - IDX_HD - IDX_HEADS - 2 * GDN_HEADS), F32)], axis=1)
    lane_vec = lambda v, off: jnp.zeros((1, LANES), F32).at[0, off:off + v.shape[0]].set(v.astype(F32))
    alog_v = lane_vec(a_log[0], S_DECAY)
    dtb_v = lane_vec(dt_bias[0], S_DECAY)

    hp = jnp.concatenate([jnp.zeros((B, PAD_ROWS, D), x.dtype),
                          jnp.broadcast_to(meta_tokens.astype(x.dtype)[None], (B, N_META, D)), x], axis=1)
    R = B * TP
    P2, Ps2 = _proj(hp.reshape(R, D), pre_mix_norm[0][None], w_main, w_small,
                    _largest_tile(R, PROJ_ROW_TILES), PROJ_COL_TILE)
    P = P2.reshape(B, TP, N_MAIN)
    Ps = Ps2.reshape(B, TP, LANES)

    y_gdn = _gdn(P, Ps, conv_w[0], alog_v, dtb_v, gdn_norm[0][None])

    cos_a, sin_a, cos_i, sin_i = _rope_tables(TP, nkb * BLK)
    qa, ka, vt, qi, ki, wt = _dsa_prep(P, Ps, cos_a, sin_a, cos_i, sin_i, nkb)
    y_dsa = _dsa(qa, qi, wt, ka, vt, ki, L, topk)

    Rr = B * L
    h1, n2 = _merge(x.reshape(Rr, D), y_gdn.reshape(Rr, GDN_W), y_dsa.reshape(Rr, ATT_Q_W), P2,
                    w_branch_gdn[0].astype(BF16), w_branch_dsa[0].astype(BF16), w_out[0].astype(BF16),
                    post_mix_norm[0][None], pre_mlp_norm[0][None], L, TP)
    out = _mlp(n2, h1, w_up[0].astype(BF16), w_down[0].astype(BF16), post_mlp_norm[0][None],
               _largest_tile(Rr, MLP_ROW_TILES), MLP_FF_TILE)
    return out.reshape(B, L, D)
```

```python
import functools
import math

import jax
import jax.numpy as jnp
import numpy as np
from jax import lax
from jax.experimental import pallas as pl
from jax.experimental.pallas import tpu as pltpu

F32 = jnp.float32
BF16 = jnp.bfloat16
I32 = jnp.int32

D_MODEL = 1024
N_META = 16
GDN_HEADS = 8
GDN_DK = 128
GDN_DV = 128
CONV_WIDTH = 4
ATT_HEADS = 8
ATT_KV_HEADS = 2
ATT_HD = 128
ATT_GROUP = ATT_HEADS // ATT_KV_HEADS
IDX_HEADS = 8
IDX_HD = 64
TOPK_MAX = 256
ROPE_THETA = 10000.0
D_FF = 4 * D_MODEL
EPS = 1e-6

GDN_W = GDN_HEADS * GDN_DK
ATT_Q_W = ATT_HEADS * ATT_HD
ATT_KV_W = ATT_KV_HEADS * ATT_HD
IDX_Q_W = IDX_HEADS * IDX_HD

LANES = 128
SUBLANES = 8
BLK = 128
PAD_ROWS = BLK - N_META
KEY_TILE = 256
WIDE_TILE = 512
VMEM_LIMIT = 56 * 1024 * 1024
PROJ_ROW_TILES = (1024, 768, 512, 384, 256, 128)
PROJ_COL_TILE = 1024
MLP_ROW_TILES = (1024, 512, 256, 128)
MLP_FF_TILE = 512
MERGE_BLOCKS = 4
PREV_ROWS = 16

C_GQ, C_GK, C_GV, C_GZ = 0, 1024, 2048, 3072
C_AQ, C_AK, C_AV, C_IQ = 4096, 5120, 5376, 5632
C_GA, C_GB = 6144, 7168
N_MAIN = 8192
S_IK, S_IW, S_BETA, S_DECAY = 0, 64, 72, 80

NEG_BIG = -1e30
PROBES_PER_CHECK = 4
FAST_PROBES = 28


def _cparams(sem):
    return pltpu.CompilerParams(dimension_semantics=sem, vmem_limit_bytes=VMEM_LIMIT)


def _split3(a):
    h = a.astype(BF16)
    r = a - h.astype(F32)
    m = r.astype(BF16)
    l = (r - m.astype(F32)).astype(BF16)
    return h, m, l


def _split2(a):
    h = a.astype(BF16)
    return h, (a - h.astype(F32)).astype(BF16)


def _rms(x, g):
    return x * lax.rsqrt(jnp.mean(x * x, axis=-1, keepdims=True) + EPS) * g


def _nt_dot(a, b):
    return lax.dot_general(a, b, (((1,), (1,)), ((), ())), preferred_element_type=F32)


def _proj_kernel(h_ref, g_ref, w_ref, ws_ref, o_ref, os_ref, n_ref):
    j = pl.program_id(1)

    @pl.when(j == 0)
    def _():
        n = _rms(h_ref[...], g_ref[...])
        n_ref[...] = n.astype(BF16)
        nh, nl = _split2(n)
        wh, wl = _split2(ws_ref[...])
        os_ref[...] = (jnp.dot(nh, wh, preferred_element_type=F32)
                       + jnp.dot(nl, wh, preferred_element_type=F32)
                       + jnp.dot(nh, wl, preferred_element_type=F32))

    o_ref[...] = jnp.dot(n_ref[...], w_ref[...], preferred_element_type=F32).astype(o_ref.dtype)


def _proj(hp, g, w_main, w_small, tm, tn):
    R = hp.shape[0]
    return pl.pallas_call(
        _proj_kernel,
        grid=(R // tm, N_MAIN // tn),
        in_specs=[
            pl.BlockSpec((tm, D_MODEL), lambda i, j: (i, 0)),
            pl.BlockSpec((1, D_MODEL), lambda i, j: (0, 0)),
            pl.BlockSpec((D_MODEL, tn), lambda i, j: (0, j)),
            pl.BlockSpec((D_MODEL, LANES), lambda i, j: (0, 0)),
        ],
        out_specs=[
            pl.BlockSpec((tm, tn), lambda i, j: (i, j)),
            pl.BlockSpec((tm, LANES), lambda i, j: (i, 0)),
        ],
        out_shape=[
            jax.ShapeDtypeStruct((R, N_MAIN), BF16),
            jax.ShapeDtypeStruct((R, LANES), F32),
        ],
        scratch_shapes=[pltpu.VMEM((tm, D_MODEL), BF16)],
        compiler_params=_cparams(("parallel", "arbitrary")),
        name="in_proj",
    )(hp, g, w_main, w_small)


def _mm3(a_parts, b_parts):
    ah, al = a_parts
    bh, bl = b_parts
    lhs = jnp.concatenate([ah, al, ah], axis=1)
    rhs = jnp.concatenate([bh, bh, bl], axis=0)
    return jnp.dot(lhs, rhs, preferred_element_type=F32)


def _unit_lower_inverses(a_list, eye):
    heads = range(len(a_list))
    sp = [_split2(a) for a in a_list]
    p = [_mm3(sp[h], sp[h]) for h in heads]
    x = [eye - a for a in a_list]
    for _ in range(5):
        xs = [_split2(v) for v in x]
        ps = [_split2(v) for v in p]
        r = [_mm3((jnp.concatenate([xs[h][0], ps[h][0]], axis=0), jnp.concatenate([xs[h][1], ps[h][1]], axis=0)),
                  ps[h]) for h in heads]
        x = [x[h] + r[h][:BLK] for h in heads]
        p = [r[h][BLK:] for h in heads]
    xs = [_split2(v) for v in x]
    ps = [_split2(v) for v in p]
    return [x[h] + _mm3(xs[h], ps[h]) for h in heads]


def _gdn_kernel(x_ref, xp_ref, ps_ref, z_ref, cw_ref, alog_ref, dtb_ref, nw_ref, o_ref, s_ref, buf_ref):
    n = pl.program_id(1)
    heads = range(GDN_HEADS)

    @pl.when(n == 0)
    def _():
        s_ref[...] = jnp.zeros_like(s_ref)

    prev = xp_ref[0, PREV_ROWS - SUBLANES:, :].astype(F32)
    buf_ref[0:SUBLANES, :] = jnp.where(n == 0, jnp.zeros_like(prev), prev)
    buf_ref[SUBLANES:SUBLANES + BLK, :] = x_ref[0].astype(F32)
    first = SUBLANES - (CONV_WIDTH - 1)
    y = buf_ref[pl.ds(first, BLK), :] * cw_ref[0:1, :]
    for t in range(1, CONV_WIDTH):
        y = y + buf_ref[pl.ds(first + t, BLK), :] * cw_ref[t:t + 1, :]
    y = y * jax.nn.sigmoid(y)

    s = ps_ref[0]
    row = lax.broadcasted_iota(I32, (BLK, LANES), 0) + n * BLK
    lane = lax.broadcasted_iota(I32, (BLK, LANES), 1)
    xs = s + dtb_ref[...]
    softplus = jnp.maximum(xs, 0.0) + jnp.log(1.0 + jnp.exp(-jnp.abs(xs)))
    gb = jnp.where((lane >= S_DECAY) & (lane < S_DECAY + GDN_HEADS), -jnp.exp(alog_ref[...]) * softplus,
                   jnp.where((lane >= S_BETA) & (lane < S_BETA + GDN_HEADS), jax.nn.sigmoid(s), 0.0))
    gb = jnp.where(row >= PAD_ROWS, gb, 0.0)

    ri = lax.broadcasted_iota(I32, (BLK, BLK), 0)
    ci = lax.broadcasted_iota(I32, (BLK, BLK), 1)
    tri_incl = ri >= ci
    tri_strict = ri > ci
    eye = (ri == ci).astype(F32)
    ones_l = tri_incl.astype(BF16)

    gh, gm, gl = _split3(gb)
    gc = jnp.dot(jnp.concatenate([ones_l, ones_l, ones_l], axis=1),
                 jnp.concatenate([gh, gm, gl], axis=0), preferred_element_type=F32)
    sel = (lax.broadcasted_iota(I32, (16, LANES), 1)
           == lax.broadcasted_iota(I32, (16, LANES), 0) + S_DECAY).astype(BF16)
    ch, cm, cl = _split3(gc)
    gct = _nt_dot(jnp.concatenate([sel, sel, sel], axis=1), jnp.concatenate([ch, cm, cl], axis=1))

    def head_cols(base, h):
        return y[:, base + h * GDN_DK:base + (h + 1) * GDN_DK]

    q = [head_cols(C_GQ, h) for h in heads]
    k = [head_cols(C_GK, h) for h in heads]
    v = [head_cols(C_GV, h) for h in heads]
    q = [a * lax.rsqrt(jnp.sum(a * a, axis=-1, keepdims=True) + EPS) * (GDN_DK ** -0.5) for a in q]
    k = [a * lax.rsqrt(jnp.sum(a * a, axis=-1, keepdims=True) + EPS) for a in k]
    beta = [gb[:, S_BETA + h:S_BETA + h + 1] for h in heads]
    gcc = [gc[:, S_DECAY + h:S_DECAY + h + 1] for h in heads]
    g_last = [c[BLK - 1:BLK, :] for c in gcc]
    decay = [jnp.where(tri_incl, jnp.exp(jnp.minimum(gcc[h] - gct[h:h + 1, :], 0.0)), 0.0) for h in heads]
    e_gc = [jnp.exp(c) for c in gcc]
    kb = [k[h] * beta[h] for h in heads]
    kbf = [a.astype(BF16) for a in k]
    kk = [_nt_dot(kb[h].astype(BF16), kbf[h]) for h in heads]
    qk = [_nt_dot(q[h].astype(BF16), kbf[h]) for h in heads]
    tm = _unit_lower_inverses([jnp.where(tri_strict, kk[h] * decay[h], 0.0) for h in heads], eye)
    uw = [jnp.dot(tm[h].astype(BF16), jnp.concatenate([v[h] * beta[h], kb[h] * e_gc[h]], axis=1).astype(BF16),
                  preferred_element_type=F32) for h in heads]
    attn = [(qk[h] * decay[h]).astype(BF16) for h in heads]
    q_dec = [(q[h] * e_gc[h]).astype(BF16) for h in heads]
    k_tail = [(k[h] * jnp.exp(g_last[h] - gcc[h])).T.astype(BF16) for h in heads]

    s_old = [s_ref[h] for h in heads]
    s_bf = [a.astype(BF16) for a in s_old]
    v_new = [uw[h][:, :GDN_DV] - jnp.dot(uw[h][:, GDN_DV:].astype(BF16), s_bf[h], preferred_element_type=F32)
             for h in heads]
    vn_bf = [a.astype(BF16) for a in v_new]
    o = [jnp.dot(q_dec[h], s_bf[h], preferred_element_type=F32)
         + jnp.dot(attn[h], vn_bf[h], preferred_element_type=F32) for h in heads]
    for h in heads:
        s_ref[h] = s_old[h] * jnp.exp(g_last[h]) + jnp.dot(k_tail[h], vn_bf[h], preferred_element_type=F32)
    for h in heads:
        z = z_ref[0, :, h * GDN_DV:(h + 1) * GDN_DV].astype(F32)
        o_ref[0, :, h * GDN_DV:(h + 1) * GDN_DV] = (_rms(o[h], nw_ref[...]) * (z * jax.nn.sigmoid(z))).astype(o_ref.dtype)


def _gdn(P, Ps, conv_w, alog_v, dtb_v, norm_w):
    B, TP, _ = P.shape
    nb = TP // BLK
    L = TP - BLK
    wq = 3 * GDN_W
    const = lambda b, n: (0, 0)
    return pl.pallas_call(
        _gdn_kernel,
        grid=(B, nb),
        in_specs=[
            pl.BlockSpec((1, BLK, wq), lambda b, n: (b, n, 0)),
            pl.BlockSpec((1, PREV_ROWS, wq), lambda b, n: (b, jnp.maximum(n * (BLK // PREV_ROWS) - 1, 0), 0)),
            pl.BlockSpec((1, BLK, LANES), lambda b, n: (b, n, 0)),
            pl.BlockSpec((1, BLK, GDN_W), lambda b, n: (b, n, C_GZ // GDN_W)),
            pl.BlockSpec((CONV_WIDTH, wq), const),
            pl.BlockSpec((1, LANES), const),
            pl.BlockSpec((1, LANES), const),
            pl.BlockSpec((1, GDN_DV), const),
        ],
        out_specs=pl.BlockSpec((1, BLK, GDN_W), lambda b, n: (b, jnp.maximum(n - 1, 0), 0)),
        out_shape=jax.ShapeDtypeStruct((B, L, GDN_W), BF16),
        scratch_shapes=[pltpu.VMEM((GDN_HEADS, GDN_DK, GDN_DV), F32), pltpu.VMEM((SUBLANES + BLK, wq), F32)],
        compiler_params=_cparams(("parallel", "arbitrary")),
        name="gdn_scan",
    )(P, P, Ps, P, conv_w, alog_v, dtb_v, norm_w)


def _rope128(x, cos, sin):
    return x * cos + pltpu.roll(x, ATT_HD // 2, 1) * sin


def _rope64(x, cos, sin, first_half):
    rot = jnp.where(first_half, pltpu.roll(x, LANES - IDX_HD // 2, 1), pltpu.roll(x, IDX_HD // 2, 1))
    return x * cos + rot * sin


def _dsa_prep_kernel(aq_ref, akv_ref, iq_ref, s_ref, ca_ref, sa_ref, ci_ref, si_ref,
                     qa_ref, ka_ref, vt_ref, qi_ref, ki_ref, wt_ref, *, n_real, q_scale):
    n = pl.program_id(1)
    live = n < n_real
    ca, sa, ci, si = ca_ref[...], sa_ref[...], ci_ref[...], si_ref[...]
    lane = lax.broadcasted_iota(I32, (BLK, LANES), 1)
    first_half = (lane % IDX_HD) < (IDX_HD // 2)
    zero = jnp.zeros((BLK, LANES), F32)

    for h in range(ATT_HEADS):
        sl = slice(h * ATT_HD, (h + 1) * ATT_HD)
        qa_ref[0, :, sl] = (_rope128(aq_ref[0, :, sl].astype(F32), ca, sa) * q_scale).astype(BF16)
    for g in range(ATT_KV_HEADS):
        sl = slice(g * ATT_HD, (g + 1) * ATT_HD)
        k = _rope128(akv_ref[0, :, sl].astype(F32), ca, sa)
        ka_ref[0, :, sl] = jnp.where(live, k, zero).astype(BF16)
        v = akv_ref[0, :, ATT_KV_W + g * ATT_HD:ATT_KV_W + (g + 1) * ATT_HD].astype(F32)
        vt_ref[0, 0, g, :ATT_HD, :] = jnp.where(live, v, zero).T.astype(BF16)
        vt_ref[0, 0, g, ATT_HD:, :] = jnp.ones((SUBLANES, BLK), BF16)
    for p in range(IDX_Q_W // LANES):
        sl = slice(p * LANES, (p + 1) * LANES)
        qi_ref[0, :, sl] = _rope64(iq_ref[0, :, sl].astype(F32), ci, si, first_half).astype(BF16)
    s = s_ref[0]
    ki = _rope64(s, ci, si, first_half)
    ki_ref[0] = jnp.where(live, ki, zero)[:, :IDX_HD].astype(BF16)
    wsc = s * ((IDX_HEADS * IDX_HD) ** -0.5)
    wh, wm, wl = _split3(wsc)
    sel = (lax.broadcasted_iota(I32, (16, LANES), 1)
           == lax.broadcasted_iota(I32, (16, LANES), 0) + S_IW).astype(BF16)
    wt = _nt_dot(jnp.concatenate([sel, sel, sel], axis=1), jnp.concatenate([wh, wm, wl], axis=1))
    wt_ref[0] = wt[:IDX_HEADS]


def _dsa_prep(P, Ps, cos_a, sin_a, cos_i, sin_i, nkb):
    B, TP, _ = P.shape
    n_real = TP // BLK
    TK = nkb * BLK
    cl = lambda b, n: (b, jnp.minimum(n, n_real - 1), 0)
    tab = lambda b, n: (jnp.minimum(n, n_real - 1), 0)
    return pl.pallas_call(
        functools.partial(_dsa_prep_kernel, n_real=n_real, q_scale=ATT_HD ** -0.5 * math.log2(math.e)),
        grid=(B, nkb),
        in_specs=[
            pl.BlockSpec((1, BLK, ATT_Q_W), lambda b, n: (b, jnp.minimum(n, n_real - 1), C_AQ // ATT_Q_W)),
            pl.BlockSpec((1, BLK, 2 * ATT_KV_W), lambda b, n: (b, jnp.minimum(n, n_real - 1), C_AK // (2 * ATT_KV_W))),
            pl.BlockSpec((1, BLK, IDX_Q_W), lambda b, n: (b, jnp.minimum(n, n_real - 1), C_IQ // IDX_Q_W)),
            pl.BlockSpec((1, BLK, LANES), cl),
            pl.BlockSpec((BLK, LANES), tab),
            pl.BlockSpec((BLK, LANES), tab),
            pl.BlockSpec((BLK, LANES), tab),
            pl.BlockSpec((BLK, LANES), tab),
        ],
        out_specs=[
            pl.BlockSpec((1, BLK, ATT_Q_W), lambda b, n: (b, n, 0)),
            pl.BlockSpec((1, BLK, ATT_KV_W), lambda b, n: (b, n, 0)),
            pl.BlockSpec((1, 1, ATT_KV_HEADS, ATT_HD + SUBLANES, BLK),
                         lambda b, n: (b, n // (WIDE_TILE // BLK), 0, 0, n % (WIDE_TILE // BLK))),
            pl.BlockSpec((1, BLK, IDX_Q_W), lambda b, n: (b, n, 0)),
            pl.BlockSpec((1, BLK, IDX_HD), lambda b, n: (b, n, 0)),
            pl.BlockSpec((1, IDX_HEADS, BLK), lambda b, n: (b, 0, n)),
        ],
        out_shape=[
            jax.ShapeDtypeStruct((B, TK, ATT_Q_W), BF16),
            jax.ShapeDtypeStruct((B, TK, ATT_KV_W), BF16),
            jax.ShapeDtypeStruct((B, TK // WIDE_TILE, ATT_KV_HEADS, ATT_HD + SUBLANES, WIDE_TILE), BF16),
            jax.ShapeDtypeStruct((B, TK, IDX_Q_W), BF16),
            jax.ShapeDtypeStruct((B, TK, IDX_HD), BF16),
            jax.ShapeDtypeStruct((B, IDX_HEADS, TK), F32),
        ],
        compiler_params=_cparams(("parallel", "arbitrary")),
        name="dsa_prep",
    )(P, P, P, Ps, cos_a, sin_a, cos_i, sin_i)


def _colsum(x):
    return jnp.sum(x, axis=0, keepdims=True, dtype=x.dtype)


def _fold8(x, op):
    return op(x.reshape(KEY_TILE // SUBLANES, SUBLANES, BLK), axis=0)


def _dsa_kernel(qa_ref, qi_ref, wt_ref, ka_ref, vt_ref, ki_ref, o_ref,
                s_ref, bias_ref, acc_ref, x_ref, *, topk):
    i = pl.program_id(1)
    q_row0 = (i + 1) * BLK
    nt = (i + 3) // 2
    kt = KEY_TILE
    qrow = lax.broadcasted_iota(I32, (kt, BLK), 1) + q_row0

    def rows(t):
        return pl.ds(pl.multiple_of(t * kt, kt), kt)

    qi = qi_ref[0]
    qs = jnp.concatenate([qi[:, h * IDX_HD:(h + 1) * IDX_HD] for h in range(IDX_HEADS)], axis=0)
    wt = wt_ref[0]
    nw = (nt + 1) // 2

    def score_tile(u, carry):
        lo8, hi8 = carry
        r0 = pl.multiple_of(u * WIDE_TILE, WIDE_TILE)
        d = _nt_dot(ki_ref[0, pl.ds(r0, WIDE_TILE), :], qs)
        for half in range(WIDE_TILE // kt):
            s = jnp.zeros((kt, BLK), F32)
            for h in range(IDX_HEADS):
                s = s + jnp.maximum(d[half * kt:(half + 1) * kt, h * BLK:(h + 1) * BLK], 0.0) * wt[h:h + 1, :]
            krow = lax.broadcasted_iota(I32, (kt, BLK), 0) + (r0 + half * kt)
            valid = (krow >= PAD_ROWS) & (krow <= qrow)
            s_ref[pl.ds(r0 + half * kt, kt), :] = jnp.where(valid, s, -jnp.inf)
            lo8 = jnp.minimum(lo8, _fold8(s, jnp.min))
            hi8 = jnp.maximum(hi8, _fold8(s, jnp.max))
        return lo8, hi8

    lo8, hi8 = lax.fori_loop(0, nw, score_tile,
                             (jnp.full((SUBLANES, BLK), jnp.inf, F32), jnp.full((SUBLANES, BLK), -jnp.inf, F32)))

    def count_pass(p):
        def body(t, acc):
            ge = s_ref[rows(t), :] >= p
            return acc + _fold8(jnp.where(ge, 1, 0).astype(I32), functools.partial(jnp.sum, dtype=I32))
        return _colsum(lax.fori_loop(0, nt, body, jnp.zeros((SUBLANES, BLK), I32)))

    def snap_pass(p):
        def body(t, carry):
            acc, up, dn = carry
            s = s_ref[rows(t), :]
            ge = s >= p
            acc = acc + _fold8(jnp.where(ge, 1, 0).astype(I32), functools.partial(jnp.sum, dtype=I32))
            up = jnp.minimum(up, _fold8(jnp.where(ge, s, jnp.inf), jnp.min))
            dn = jnp.maximum(dn, _fold8(jnp.where(ge, -jnp.inf, s), jnp.max))
            return acc, up, dn
        acc, up, dn = lax.fori_loop(0, nt, body, (jnp.zeros((SUBLANES, BLK), I32),
                                                  jnp.full((SUBLANES, BLK), jnp.inf, F32),
                                                  jnp.full((SUBLANES, BLK), -jnp.inf, F32)))
        return _colsum(acc), jnp.min(up, axis=0, keepdims=True), jnp.max(dn, axis=0, keepdims=True)

    def n_active(clo, lo, hi):
        return jnp.max(jnp.where((clo > topk) & (lo < hi), 1, 0).astype(I32))

    lo = jnp.min(lo8, axis=0, keepdims=True)
    hi = jnp.max(hi8, axis=0, keepdims=True)
    clo = lax.broadcasted_iota(I32, (1, BLK), 1) + (q_row0 - PAD_ROWS + 1)
    chi = jnp.zeros((1, BLK), I32)

    def fast_cond(c):
        it, lo, hi, clo, chi = c
        return (it < FAST_PROBES) & (n_active(clo, lo, hi) > 0)

    def fast_body(c):
        it, lo, hi, clo, chi = c
        for _ in range(PROBES_PER_CHECK):
            p = 0.5 * lo + 0.5 * hi
            cnt = count_pass(p)
            act = (clo > topk) & (p > lo) & (p < hi)
            up = act & (cnt >= topk)
            dn = act & (cnt < topk)
            lo, hi, clo, chi = jnp.where(up, p, lo), jnp.where(dn, p, hi), jnp.where(up, cnt, clo), jnp.where(dn, cnt, chi)
        return it + PROBES_PER_CHECK, lo, hi, clo, chi

    _, lo, hi, clo, chi = lax.while_loop(fast_cond, fast_body, (jnp.int32(0), lo, hi, clo, chi))

    def slow_cond(c):
        lo, hi, clo, chi = c
        return n_active(clo, lo, hi) > 0

    def slow_body(c):
        lo, hi, clo, chi = c
        mid = 0.5 * lo + 0.5 * hi
        p = jnp.where(mid > lo, jnp.minimum(mid, hi), hi)
        cnt, nxt, prv = snap_pass(p)
        act = (clo > topk) & (lo < hi)
        up = act & (cnt >= topk)
        dn = act & (cnt < topk)
        return (jnp.where(up, nxt, lo), jnp.where(dn, prv, hi), jnp.where(up, cnt, clo), jnp.where(dn, cnt, chi))

    lo, hi, clo, chi = lax.while_loop(slow_cond, slow_body, (lo, hi, clo, chi))
    thr = lo
    need = (topk - chi).astype(F32)
    any_tie = jnp.max(jnp.where(clo > topk, 1, 0).astype(I32))

    def plain_bias(t, carry):
        bias_ref[rows(t), :] = jnp.where(s_ref[rows(t), :] >= thr, 0.0, NEG_BIG)
        return carry

    def tie_bias(t, run):
        s = s_ref[rows(t), :]
        eq = (s == thr) & (clo > topk)
        lower = (lax.broadcasted_iota(I32, (kt, kt), 0) >= lax.broadcasted_iota(I32, (kt, kt), 1)).astype(BF16)
        e = jnp.where(eq, 1.0, 0.0)
        rank = jnp.dot(lower, e.astype(BF16), preferred_element_type=F32) + run
        keep = (eq & (rank <= need)) | (jnp.logical_not(eq) & (s >= thr))
        bias_ref[rows(t), :] = jnp.where(keep, 0.0, NEG_BIG)
        return run + jnp.sum(e, axis=0, keepdims=True)

    @pl.when(any_tie == 0)
    def _():
        lax.fori_loop(0, 2 * nw, plain_bias, 0)

    @pl.when(any_tie != 0)
    def _():
        lax.fori_loop(0, 2 * nw, tie_bias, jnp.zeros((1, BLK), F32))

    gw = ATT_GROUP * BLK
    acc_ref[...] = jnp.zeros_like(acc_ref)

    def att_tile(u, m):
        r0 = pl.multiple_of(u * WIDE_TILE, WIDE_TILE)
        b = bias_ref[pl.ds(r0, WIDE_TILE), :]
        b4 = jnp.concatenate([b] * ATT_GROUP, axis=1)
        for g in range(ATT_KV_HEADS):
            qg = jnp.concatenate([qa_ref[0, :, (g * ATT_GROUP + r) * ATT_HD:(g * ATT_GROUP + r + 1) * ATT_HD]
                                  for r in range(ATT_GROUP)], axis=0)
            x_ref[g] = _nt_dot(ka_ref[0, pl.ds(r0, WIDE_TILE), g * ATT_HD:(g + 1) * ATT_HD], qg)
        m_out = []
        for g in range(ATT_KV_HEADS):
            x = x_ref[g] + b4
            m_new = jnp.maximum(m[g], jnp.max(x, axis=0, keepdims=True))
            alpha = jnp.exp2(m[g] - m_new)
            p = jnp.exp2(x - m_new).astype(BF16)
            acc_ref[g] = acc_ref[g] * alpha + jnp.dot(vt_ref[0, u, g], p, preferred_element_type=F32)
            m_out.append(m_new)
        return tuple(m_out)

    lax.fori_loop(0, nw, att_tile, tuple(jnp.full((1, gw), NEG_BIG, F32) for _ in range(ATT_KV_HEADS)))
    for g in range(ATT_KV_HEADS):
        acc = acc_ref[g]
        out_t = acc[:ATT_HD] / acc[ATT_HD:ATT_HD + 1]
        for r in range(ATT_GROUP):
            hh = g * ATT_GROUP + r
            o_ref[0, :, hh * ATT_HD:(hh + 1) * ATT_HD] = out_t[:, r * BLK:(r + 1) * BLK].T.astype(o_ref.dtype)


def _dsa(qa, qi, wt, ka, vt, ki, L, topk):
    B, TK, _ = qa.shape
    nq = L // BLK
    return pl.pallas_call(
        functools.partial(_dsa_kernel, topk=topk),
        grid=(B, nq),
        in_specs=[
            pl.BlockSpec((1, BLK, ATT_Q_W), lambda b, i: (b, i + 1, 0)),
            pl.BlockSpec((1, BLK, IDX_Q_W), lambda b, i: (b, i + 1, 0)),
            pl.BlockSpec((1, IDX_HEADS, BLK), lambda b, i: (b, 0, i + 1)),
            pl.BlockSpec((1, TK, ATT_KV_W), lambda b, i: (b, 0, 0)),
            pl.BlockSpec((1, TK // WIDE_TILE, ATT_KV_HEADS, ATT_HD + SUBLANES, WIDE_TILE), lambda b, i: (b, 0, 0, 0, 0)),
            pl.BlockSpec((1, TK, IDX_HD), lambda b, i: (b, 0, 0)),
        ],
        out_specs=pl.BlockSpec((1, BLK, ATT_Q_W), lambda b, i: (b, i, 0)),
        out_shape=jax.ShapeDtypeStruct((B, L, ATT_Q_W), BF16),
        scratch_shapes=[pltpu.VMEM((TK, BLK), F32), pltpu.VMEM((TK, BLK), F32),
                        pltpu.VMEM((ATT_KV_HEADS, ATT_HD + SUBLANES, ATT_GROUP * BLK), F32),
                        pltpu.VMEM((ATT_KV_HEADS, WIDE_TILE, ATT_GROUP * BLK), F32)],
        compiler_params=_cparams(("parallel", "arbitrary")),
        name="dsa_attn",
    )(qa, qi, wt, ka, vt, ki)


def _merge_kernel(*refs):
    x_ref, yg_ref, yd_ref = refs[:3]
    gate_refs = refs[3:3 + 2 * MERGE_BLOCKS]
    wg_ref, wd_ref, wo_ref, npost_ref, npre_ref, h_ref, n2_ref = refs[3 + 2 * MERGE_BLOCKS:]
    gate = lambda blocks: jnp.concatenate([jax.nn.sigmoid(r[...].astype(F32)) for r in blocks], axis=0)
    bg = jnp.dot(yg_ref[...], wg_ref[...], preferred_element_type=F32)
    bd = jnp.dot(yd_ref[...], wd_ref[...], preferred_element_type=F32)
    merged = gate(gate_refs[:MERGE_BLOCKS]) * bg + gate(gate_refs[MERGE_BLOCKS:]) * bd
    mix = jnp.dot(merged.astype(BF16), wo_ref[...], preferred_element_type=F32)
    h1 = x_ref[...] + _rms(mix, npost_ref[...])
    h_ref[...] = h1
    n2_ref[...] = _rms(h1, npre_ref[...]).astype(BF16)


def _merge(x2, yg, yd, P2, wg, wd, wo, npost, npre, L, TP):
    R = x2.shape[0]
    tm = MERGE_BLOCKS * BLK
    assert L % tm == 0
    per = L // BLK
    def gate_spec(col, k):
        def index(i):
            blk = i * MERGE_BLOCKS + k
            return ((blk // per) * (TP // BLK) + 1 + blk % per, col // D_MODEL)
        return pl.BlockSpec((BLK, D_MODEL), index)
    row = lambda i: (i, 0)
    full = lambda i: (0, 0)
    gates = [gate_spec(C_GA, k) for k in range(MERGE_BLOCKS)] + [gate_spec(C_GB, k) for k in range(MERGE_BLOCKS)]
    return pl.pallas_call(
        _merge_kernel,
        grid=(R // tm,),
        in_specs=[pl.BlockSpec((tm, D_MODEL), row), pl.BlockSpec((tm, GDN_W), row), pl.BlockSpec((tm, ATT_Q_W), row)]
        + gates
        + [pl.BlockSpec((GDN_W, D_MODEL), full), pl.BlockSpec((ATT_Q_W, D_MODEL), full),
           pl.BlockSpec((D_MODEL, D_MODEL), full), pl.BlockSpec((1, D_MODEL), full), pl.BlockSpec((1, D_MODEL), full)],
        out_specs=[pl.BlockSpec((tm, D_MODEL), row), pl.BlockSpec((tm, D_MODEL), row)],
        out_shape=[jax.ShapeDtypeStruct((R, D_MODEL), F32), jax.ShapeDtypeStruct((R, D_MODEL), BF16)],
        compiler_params=_cparams(("parallel",)),
        name="merge_out",
    )(x2, yg, yd, *([P2] * (2 * MERGE_BLOCKS)), wg, wd, wo, npost, npre)


def _mlp_kernel(n_ref, h_ref, wu_ref, wd_ref, g_ref, o_ref, acc_ref):
    j = pl.program_id(1)

    @pl.when(j == 0)
    def _():
        acc_ref[...] = jnp.zeros_like(acc_ref)

    u = jnp.maximum(jnp.dot(n_ref[...], wu_ref[...], preferred_element_type=F32), 0.0)
    acc_ref[...] += jnp.dot((u * u).astype(BF16), wd_ref[...], preferred_element_type=F32)

    @pl.when(j == pl.num_programs(1) - 1)
    def _():
        o_ref[...] = h_ref[...] + _rms(acc_ref[...], g_ref[...])


def _mlp(n2, h1, wu, wd, g, tm, tf):
    R = n2.shape[0]
    return pl.pallas_call(
        _mlp_kernel,
        grid=(R // tm, D_FF // tf),
        in_specs=[
            pl.BlockSpec((tm, D_MODEL), lambda i, j: (i, 0)),
            pl.BlockSpec((tm, D_MODEL), lambda i, j: (i, 0)),
            pl.BlockSpec((D_MODEL, tf), lambda i, j: (0, j)),
            pl.BlockSpec((tf, D_MODEL), lambda i, j: (j, 0)),
            pl.BlockSpec((1, D_MODEL), lambda i, j: (0, 0)),
        ],
        out_specs=pl.BlockSpec((tm, D_MODEL), lambda i, j: (i, 0)),
        out_shape=jax.ShapeDtypeStruct((R, D_MODEL), F32),
        scratch_shapes=[pltpu.VMEM((tm, D_MODEL), F32)],
        compiler_params=_cparams(("parallel", "arbitrary")),
        name="mlp",
    )(n2, h1, wu, wd, g)


def _largest_tile(n, candidates):
    for c in candidates:
        if n % c == 0:
            return c
    raise ValueError(f"no tile for {n}")


def _rope_tables(tp, tk):
    pos = jnp.maximum(jnp.arange(tk, dtype=jnp.int32) - PAD_ROWS, 0)
    pos = jnp.where(jnp.arange(tk) < tp, pos, 0).astype(F32)

    def table(dim):
        inv = ROPE_THETA ** (-jnp.arange(0, dim, 2, dtype=F32) / dim)
        ang = pos[:, None] * inv[None, :]
        c, s = jnp.cos(ang), jnp.sin(ang)
        reps = LANES // dim
        return jnp.tile(jnp.concatenate([c, c], axis=1), (1, reps)), jnp.tile(jnp.concatenate([-s, s], axis=1), (1, reps))

    return table(ATT_HD) + table(IDX_HD)


def kernel(x, meta_tokens, pre_mix_norm, w_in, conv_w, a_log, dt_bias, gdn_norm, w_branch_gdn,
           w_branch_dsa, w_out, post_mix_norm, pre_mlp_norm, w_up, w_down, post_mlp_norm):
    B, L, D = x.shape
    assert D == D_MODEL and L % BLK == 0 and w_in.shape[0] == 1, "single layer, 128-aligned sequence"
    TP = BLK + L
    per_wide = WIDE_TILE // BLK
    nkb = -(-(TP // BLK) // per_wide) * per_wide
    topk = min(TOPK_MAX, L // 4)

    w = w_in[0]
    o = np.cumsum((0, GDN_W, GDN_W, GDN_W, GDN_W, GDN_HEADS, GDN_HEADS, ATT_Q_W, ATT_KV_W, ATT_KV_W,
                   IDX_Q_W, IDX_HD, IDX_HEADS, D_MODEL, D_MODEL))
    seg = lambda i: w[:, o[i]:o[i + 1]]
    w_main = jnp.concatenate([seg(0), seg(1), seg(2), seg(3), seg(6), seg(7), seg(8), seg(9), seg(12), seg(13)],
                             axis=1).astype(BF16)
    w_small = jnp.concatenate([seg(10), seg(11), seg(4), seg(5),
                               jnp.zeros((D, LANES - IDX_HD - IDX_HEADS - 2 * GDN_HEADS), F32)], axis=1)
    lane_vec = lambda v, off: jnp.zeros((1, LANES), F32).at[0, off:off + v.shape[0]].set(v.astype(F32))
    alog_v = lane_vec(a_log[0], S_DECAY)
    dtb_v = lane_vec(dt_bias[0], S_DECAY)

    hp = jnp.concatenate([jnp.zeros((B, PAD_ROWS, D), x.dtype),
                          jnp.broadcast_to(meta_tokens.astype(x.dtype)[None], (B, N_META, D)), x], axis=1)
    R = B * TP
    P2, Ps2 = _proj(hp.reshape(R, D), pre_mix_norm[0][None], w_main, w_small,
                    _largest_tile(R, PROJ_ROW_TILES), PROJ_COL_TILE)
    P = P2.reshape(B, TP, N_MAIN)
    Ps = Ps2.reshape(B, TP, LANES)

    y_gdn = _gdn(P, Ps, conv_w[0], alog_v, dtb_v, gdn_norm[0][None])

    cos_a, sin_a, cos_i, sin_i = _rope_tables(TP, nkb * BLK)
    qa, ka, vt, qi, ki, wt = _dsa_prep(P, Ps, cos_a, sin_a, cos_i, sin_i, nkb)
    y_dsa = _dsa(qa, qi, wt, ka, vt, ki, L, topk)

    Rr = B * L
    h1, n2 = _merge(x.reshape(Rr, D), y_gdn.reshape(Rr, GDN_W), y_dsa.reshape(Rr, ATT_Q_W), P2,
                    w_branch_gdn[0].astype(BF16), w_branch_dsa[0].astype(BF16), w_out[0].astype(BF16),
                    post_mix_norm[0][None], pre_mlp_norm[0][None], L, TP)
    out = _mlp(n2, h1, w_up[0].astype(BF16), w_down[0].astype(BF16), post_mlp_norm[0][None],
               _largest_tile(Rr, MLP_ROW_TILES), MLP_FF_TILE)
    return out.reshape(B, L, D)
```

```python
import functools
import math

import jax
import jax.numpy as jnp
import numpy as np
from jax import lax
from jax.experimental import pallas as pl
from jax.experimental.pallas import tpu as pltpu

F32 = jnp.float32
BF16 = jnp.bfloat16
I32 = jnp.int32

D_MODEL = 1024
N_META = 16
GDN_HEADS = 8
GDN_DK = 128
GDN_DV = 128
CONV_WIDTH = 4
ATT_HEADS = 8
ATT_KV_HEADS = 2
ATT_HD = 128
ATT_GROUP = ATT_HEADS // ATT_KV_HEADS
IDX_HEADS = 8
IDX_HD = 64
TOPK_MAX = 256
ROPE_THETA = 10000.0
D_FF = 4 * D_MODEL
EPS = 1e-6

GDN_W = GDN_HEADS * GDN_DK
ATT_Q_W = ATT_HEADS * ATT_HD
ATT_KV_W = ATT_KV_HEADS * ATT_HD
IDX_Q_W = IDX_HEADS * IDX_HD

LANES = 128
SUBLANES = 8
BLK = 128
PAD_ROWS = BLK - N_META
KEY_TILE = 256
WIDE_TILE = 1024
VMEM_LIMIT = 56 * 1024 * 1024
PROJ_ROW_TILES = (1024, 768, 512, 384, 256, 128)
PROJ_COL_TILE = 1024
MLP_ROW_TILES = (1024, 512, 256, 128)
MLP_FF_TILE = 512
GDN_HEAD_GROUP = 8
MERGE_BLOCKS = 4
PREV_ROWS = 16

C_GQ, C_GK, C_GV, C_GZ = 0, 1024, 2048, 3072
C_AQ, C_AK, C_AV, C_IQ = 4096, 5120, 5376, 5632
C_GA, C_GB = 6144, 7168
N_MAIN = 8192
S_IK, S_IW, S_BETA, S_DECAY = 0, 64, 72, 80

NEG_BIG = -1e30
BLIND_PROBES = 16
PROBES_PER_CHECK = 4
FAST_PROBES = 20


def _cparams(sem):
    return pltpu.CompilerParams(dimension_semantics=sem, vmem_limit_bytes=VMEM_LIMIT)


def _split3(a):
    h = a.astype(BF16)
    r = a - h.astype(F32)
    m = r.astype(BF16)
    l = (r - m.astype(F32)).astype(BF16)
    return h, m, l


def _split2(a):
    h = a.astype(BF16)
    return h, (a - h.astype(F32)).astype(BF16)


def _rms(x, g):
    return x * lax.rsqrt(jnp.mean(x * x, axis=-1, keepdims=True) + EPS) * g


def _nt_dot(a, b):
    return lax.dot_general(a, b, (((1,), (1,)), ((), ())), preferred_element_type=F32)


def _proj_kernel(h_ref, g_ref, w_ref, ws_ref, o_ref, os_ref, n_ref):
    j = pl.program_id(1)

    @pl.when(j == 0)
    def _():
        n = _rms(h_ref[...], g_ref[...])
        n_ref[...] = n.astype(BF16)
        nh, nl = _split2(n)
        wh, wl = _split2(ws_ref[...])
        os_ref[...] = (jnp.dot(nh, wh, preferred_element_type=F32)
                       + jnp.dot(nl, wh, preferred_element_type=F32)
                       + jnp.dot(nh, wl, preferred_element_type=F32))

    o_ref[...] = jnp.dot(n_ref[...], w_ref[...], preferred_element_type=F32).astype(o_ref.dtype)


def _proj(hp, g, w_main, w_small, tm, tn):
    R = hp.shape[0]
    return pl.pallas_call(
        _proj_kernel,
        grid=(R // tm, N_MAIN // tn),
        in_specs=[
            pl.BlockSpec((tm, D_MODEL), lambda i, j: (i, 0)),
            pl.BlockSpec((1, D_MODEL), lambda i, j: (0, 0)),
            pl.BlockSpec((D_MODEL, tn), lambda i, j: (0, j)),
            pl.BlockSpec((D_MODEL, LANES), lambda i, j: (0, 0)),
        ],
        out_specs=[
            pl.BlockSpec((tm, tn), lambda i, j: (i, j)),
            pl.BlockSpec((tm, LANES), lambda i, j: (i, 0)),
        ],
        out_shape=[
            jax.ShapeDtypeStruct((R, N_MAIN), BF16),
            jax.ShapeDtypeStruct((R, LANES), F32),
        ],
        scratch_shapes=[pltpu.VMEM((tm, D_MODEL), BF16)],
        compiler_params=_cparams(("parallel", "arbitrary")),
        name="in_proj",
    )(hp, g, w_main, w_small)


def _mm3(a_parts, b_parts):
    ah, al = a_parts
    bh, bl = b_parts
    lhs = jnp.concatenate([ah, al, ah], axis=1)
    rhs = jnp.concatenate([bh, bh, bl], axis=0)
    return jnp.dot(lhs, rhs, preferred_element_type=F32)


def _unit_lower_inverses(a_list, eye):
    heads = range(len(a_list))
    sp = [_split2(a) for a in a_list]
    p = [_mm3(sp[h], sp[h]) for h in heads]
    x = [eye - a for a in a_list]
    for _ in range(5):
        xs = [_split2(v) for v in x]
        ps = [_split2(v) for v in p]
        r = [_mm3((jnp.concatenate([xs[h][0], ps[h][0]], axis=0), jnp.concatenate([xs[h][1], ps[h][1]], axis=0)),
                  ps[h]) for h in heads]
        x = [x[h] + r[h][:BLK] for h in heads]
        p = [r[h][BLK:] for h in heads]
    xs = [_split2(v) for v in x]
    ps = [_split2(v) for v in p]
    return [x[h] + _mm3(xs[h], ps[h]) for h in heads]


def _gdn_kernel(x_ref, xp_ref, ps_ref, z_ref, cw_ref, alog_ref, dtb_ref, nw_ref, o_ref, s_ref, buf_ref):
    n = pl.program_id(1)
    heads = range(GDN_HEADS)

    @pl.when(n == 0)
    def _():
        s_ref[...] = jnp.zeros_like(s_ref)

    prev = xp_ref[0, PREV_ROWS - SUBLANES:, :].astype(F32)
    buf_ref[0:SUBLANES, :] = jnp.where(n == 0, jnp.zeros_like(prev), prev)
    buf_ref[SUBLANES:SUBLANES + BLK, :] = x_ref[0].astype(F32)
    first = SUBLANES - (CONV_WIDTH - 1)
    y = buf_ref[pl.ds(first, BLK), :] * cw_ref[0:1, :]
    for t in range(1, CONV_WIDTH):
        y = y + buf_ref[pl.ds(first + t, BLK), :] * cw_ref[t:t + 1, :]
    y = y * jax.nn.sigmoid(y)

    s = ps_ref[0]
    row = lax.broadcasted_iota(I32, (BLK, LANES), 0) + n * BLK
    lane = lax.broadcasted_iota(I32, (BLK, LANES), 1)
    xs = s + dtb_ref[...]
    softplus = jnp.maximum(xs, 0.0) + jnp.log(1.0 + jnp.exp(-jnp.abs(xs)))
    gb = jnp.where((lane >= S_DECAY) & (lane < S_DECAY + GDN_HEADS), -jnp.exp(alog_ref[...]) * softplus,
                   jnp.where((lane >= S_BETA) & (lane < S_BETA + GDN_HEADS), jax.nn.sigmoid(s), 0.0))
    gb = jnp.where(row >= PAD_ROWS, gb, 0.0)

    ri = lax.broadcasted_iota(I32, (BLK, BLK), 0)
    ci = lax.broadcasted_iota(I32, (BLK, BLK), 1)
    tri_incl = ri >= ci
    tri_strict = ri > ci
    eye = (ri == ci).astype(F32)
    ones_l = tri_incl.astype(BF16)

    gh, gm, gl = _split3(gb)
    gc = jnp.dot(jnp.concatenate([ones_l, ones_l, ones_l], axis=1),
                 jnp.concatenate([gh, gm, gl], axis=0), preferred_element_type=F32)
    sel = (lax.broadcasted_iota(I32, (16, LANES), 1)
           == lax.broadcasted_iota(I32, (16, LANES), 0) + S_DECAY).astype(BF16)
    ch, cm, cl = _split3(gc)
    gct = _nt_dot(jnp.concatenate([sel, sel, sel], axis=1), jnp.concatenate([ch, cm, cl], axis=1))

    def head_cols(base, h):
        return y[:, base + h * GDN_DK:base + (h + 1) * GDN_DK]

    def head_group(hs):
        q = {h: head_cols(C_GQ, h) for h in hs}
        k = {h: head_cols(C_GK, h) for h in hs}
        v = {h: head_cols(C_GV, h) for h in hs}
        q = {h: a * lax.rsqrt(jnp.sum(a * a, axis=-1, keepdims=True) + EPS) * (GDN_DK ** -0.5) for h, a in q.items()}
        k = {h: a * lax.rsqrt(jnp.sum(a * a, axis=-1, keepdims=True) + EPS) for h, a in k.items()}
        beta = {h: gb[:, S_BETA + h:S_BETA + h + 1] for h in hs}
        gcc = {h: gc[:, S_DECAY + h:S_DECAY + h + 1] for h in hs}
        g_last = {h: gcc[h][BLK - 1:BLK, :] for h in hs}
        decay = {h: jnp.where(tri_incl, jnp.exp(jnp.minimum(gcc[h] - gct[h:h + 1, :], 0.0)), 0.0) for h in hs}
        e_gc = {h: jnp.exp(gcc[h]) for h in hs}
        kb = {h: k[h] * beta[h] for h in hs}
        kbf = {h: k[h].astype(BF16) for h in hs}
        kk = {h: _nt_dot(kb[h].astype(BF16), kbf[h]) for h in hs}
        qk = {h: _nt_dot(q[h].astype(BF16), kbf[h]) for h in hs}
        tm = dict(zip(hs, _unit_lower_inverses([jnp.where(tri_strict, kk[h] * decay[h], 0.0) for h in hs], eye)))
        uw = {h: jnp.dot(tm[h].astype(BF16),
                         jnp.concatenate([v[h] * beta[h], kb[h] * e_gc[h]], axis=1).astype(BF16),
                         preferred_element_type=F32) for h in hs}
        attn = {h: (qk[h] * decay[h]).astype(BF16) for h in hs}
        q_dec = {h: (q[h] * e_gc[h]).astype(BF16) for h in hs}
        k_tail = {h: (k[h] * jnp.exp(g_last[h] - gcc[h])).T.astype(BF16) for h in hs}

        s_old = {h: s_ref[h] for h in hs}
        s_bf = {h: s_old[h].astype(BF16) for h in hs}
        v_new = {h: uw[h][:, :GDN_DV] - jnp.dot(uw[h][:, GDN_DV:].astype(BF16), s_bf[h], preferred_element_type=F32)
                 for h in hs}
        vn_bf = {h: v_new[h].astype(BF16) for h in hs}
        o = {h: jnp.dot(q_dec[h], s_bf[h], preferred_element_type=F32)
             + jnp.dot(attn[h], vn_bf[h], preferred_element_type=F32) for h in hs}
        for h in hs:
            s_ref[h] = s_old[h] * jnp.exp(g_last[h]) + jnp.dot(k_tail[h], vn_bf[h], preferred_element_type=F32)
        for h in hs:
            z = z_ref[0, :, h * GDN_DV:(h + 1) * GDN_DV].astype(F32)
            o_ref[0, :, h * GDN_DV:(h + 1) * GDN_DV] = (_rms(o[h], nw_ref[...]) * (z * jax.nn.sigmoid(z))).astype(o_ref.dtype)

    for g0 in range(0, GDN_HEADS, GDN_HEAD_GROUP):
        head_group(list(range(g0, g0 + GDN_HEAD_GROUP)))


def _gdn(P, Ps, conv_w, alog_v, dtb_v, norm_w):
    B, TP, _ = P.shape
    nb = TP // BLK
    L = TP - BLK
    wq = 3 * GDN_W
    const = lambda b, n: (0, 0)
    return pl.pallas_call(
        _gdn_kernel,
        grid=(B, nb),
        in_specs=[
            pl.BlockSpec((1, BLK, wq), lambda b, n: (b, n, 0)),
            pl.BlockSpec((1, PREV_ROWS, wq), lambda b, n: (b, jnp.maximum(n * (BLK // PREV_ROWS) - 1, 0), 0)),
            pl.BlockSpec((1, BLK, LANES), lambda b, n: (b, n, 0)),
            pl.BlockSpec((1, BLK, GDN_W), lambda b, n: (b, n, C_GZ // GDN_W)),
            pl.BlockSpec((CONV_WIDTH, wq), const),
            pl.BlockSpec((1, LANES), const),
            pl.BlockSpec((1, LANES), const),
            pl.BlockSpec((1, GDN_DV), const),
        ],
        out_specs=pl.BlockSpec((1, BLK, GDN_W), lambda b, n: (b, jnp.maximum(n - 1, 0), 0)),
        out_shape=jax.ShapeDtypeStruct((B, L, GDN_W), BF16),
        scratch_shapes=[pltpu.VMEM((GDN_HEADS, GDN_DK, GDN_DV), F32), pltpu.VMEM((SUBLANES + BLK, wq), F32)],
        compiler_params=_cparams(("parallel", "arbitrary")),
        name="gdn_scan",
    )(P, P, Ps, P, conv_w, alog_v, dtb_v, norm_w)


def _rope128(x, cos, sin):
    return x * cos + pltpu.roll(x, ATT_HD // 2, 1) * sin


def _rope64(x, cos, sin, first_half):
    rot = jnp.where(first_half, pltpu.roll(x, LANES - IDX_HD // 2, 1), pltpu.roll(x, IDX_HD // 2, 1))
    return x * cos + rot * sin


def _dsa_prep_kernel(aq_ref, akv_ref, iq_ref, s_ref, ca_ref, sa_ref, ci_ref, si_ref,
                     qa_ref, ka_ref, vt_ref, qi_ref, ki_ref, wt_ref, *, n_real, q_scale):
    n = pl.program_id(1)
    live = n < n_real
    ca, sa, ci, si = ca_ref[...], sa_ref[...], ci_ref[...], si_ref[...]
    lane = lax.broadcasted_iota(I32, (BLK, LANES), 1)
    first_half = (lane % IDX_HD) < (IDX_HD // 2)
    zero = jnp.zeros((BLK, LANES), F32)

    for h in range(ATT_HEADS):
        sl = slice(h * ATT_HD, (h + 1) * ATT_HD)
        qa_ref[0, :, sl] = (_rope128(aq_ref[0, :, sl].astype(F32), ca, sa) * q_scale).astype(BF16)
    for g in range(ATT_KV_HEADS):
        sl = slice(g * ATT_HD, (g + 1) * ATT_HD)
        k = _rope128(akv_ref[0, :, sl].astype(F32), ca, sa)
        ka_ref[0, :, sl] = jnp.where(live, k, zero).astype(BF16)
        v = akv_ref[0, :, ATT_KV_W + g * ATT_HD:ATT_KV_W + (g + 1) * ATT_HD].astype(F32)
        vt_ref[0, 0, g, :ATT_HD, :] = jnp.where(live, v, zero).T.astype(BF16)
        vt_ref[0, 0, g, ATT_HD:, :] = jnp.ones((SUBLANES, BLK), BF16)
    for p in range(IDX_Q_W // LANES):
        sl = slice(p * LANES, (p + 1) * LANES)
        qi_ref[0, :, sl] = _rope64(iq_ref[0, :, sl].astype(F32), ci, si, first_half).astype(BF16)
    s = s_ref[0]
    ki = _rope64(s, ci, si, first_half)
    ki_ref[0] = jnp.where(live, ki, zero)[:, :IDX_HD].astype(BF16)
    wsc = s * ((IDX_HEADS * IDX_HD) ** -0.5)
    wh, wm, wl = _split3(wsc)
    sel = (lax.broadcasted_iota(I32, (16, LANES), 1)
           == lax.broadcasted_iota(I32, (16, LANES), 0) + S_IW).astype(BF16)
    wt = _nt_dot(jnp.concatenate([sel, sel, sel], axis=1), jnp.concatenate([wh, wm, wl], axis=1))
    wt_ref[0] = wt[:IDX_HEADS]


def _dsa_prep(P, Ps, cos_a, sin_a, cos_i, sin_i, nkb):
    B, TP, _ = P.shape
    n_real = TP // BLK
    TK = nkb * BLK
    cl = lambda b, n: (b, jnp.minimum(n, n_real - 1), 0)
    tab = lambda b, n: (jnp.minimum(n, n_real - 1), 0)
    return pl.pallas_call(
        functools.partial(_dsa_prep_kernel, n_real=n_real, q_scale=ATT_HD ** -0.5 * math.log2(math.e)),
        grid=(B, nkb),
        in_specs=[
            pl.BlockSpec((1, BLK, ATT_Q_W), lambda b, n: (b, jnp.minimum(n, n_real - 1), C_AQ // ATT_Q_W)),
            pl.BlockSpec((1, BLK, 2 * ATT_KV_W), lambda b, n: (b, jnp.minimum(n, n_real - 1), C_AK // (2 * ATT_KV_W))),
            pl.BlockSpec((1, BLK, IDX_Q_W), lambda b, n: (b, jnp.minimum(n, n_real - 1), C_IQ // IDX_Q_W)),
            pl.BlockSpec((1, BLK, LANES), cl),
            pl.BlockSpec((BLK, LANES), tab),
            pl.BlockSpec((BLK, LANES), tab),
            pl.BlockSpec((BLK, LANES), tab),
            pl.BlockSpec((BLK, LANES), tab),
        ],
        out_specs=[
            pl.BlockSpec((1, BLK, ATT_Q_W), lambda b, n: (b, n, 0)),
            pl.BlockSpec((1, BLK, ATT_KV_W), lambda b, n: (b, n, 0)),
            pl.BlockSpec((1, 1, ATT_KV_HEADS, ATT_HD + SUBLANES, BLK),
                         lambda b, n: (b, n // (WIDE_TILE // BLK), 0, 0, n % (WIDE_TILE // BLK))),
            pl.BlockSpec((1, BLK, IDX_Q_W), lambda b, n: (b, n, 0)),
            pl.BlockSpec((1, BLK, IDX_HD), lambda b, n: (b, n, 0)),
            pl.BlockSpec((1, IDX_HEADS, BLK), lambda b, n: (b, 0, n)),
        ],
        out_shape=[
            jax.ShapeDtypeStruct((B, TK, ATT_Q_W), BF16),
            jax.ShapeDtypeStruct((B, TK, ATT_KV_W), BF16),
            jax.ShapeDtypeStruct((B, TK // WIDE_TILE, ATT_KV_HEADS, ATT_HD + SUBLANES, WIDE_TILE), BF16),
            jax.ShapeDtypeStruct((B, TK, IDX_Q_W), BF16),
            jax.ShapeDtypeStruct((B, TK, IDX_HD), BF16),
            jax.ShapeDtypeStruct((B, IDX_HEADS, TK), F32),
        ],
        compiler_params=_cparams(("parallel", "arbitrary")),
        name="dsa_prep",
    )(P, P, P, Ps, cos_a, sin_a, cos_i, sin_i)


def _colsum(x):
    return jnp.sum(x, axis=0, keepdims=True, dtype=x.dtype)


def _fold8(x, op):
    return op(x.reshape(KEY_TILE // SUBLANES, SUBLANES, BLK), axis=0)


def _dsa_kernel(qa_ref, qi_ref, wt_ref, ka_ref, vt_ref, ki_ref, o_ref,
                s_ref, bias_ref, acc_ref, x_ref, *, topk):
    i = pl.program_id(1)
    q_row0 = (i + 1) * BLK
    nt = (i + 3) // 2
    kt = KEY_TILE
    qrow = lax.broadcasted_iota(I32, (kt, BLK), 1) + q_row0

    def rows(t):
        return pl.ds(pl.multiple_of(t * kt, kt), kt)

    qi = qi_ref[0]
    qs = jnp.concatenate([qi[:, h * IDX_HD:(h + 1) * IDX_HD] for h in range(IDX_HEADS)], axis=0)
    wt = wt_ref[0]
    per_w = WIDE_TILE // kt
    nw = (nt + per_w - 1) // per_w

    def score_tile(u, carry):
        lo8, hi8 = carry
        r0 = pl.multiple_of(u * WIDE_TILE, WIDE_TILE)
        d = _nt_dot(ki_ref[0, pl.ds(r0, WIDE_TILE), :], qs)
        for half in range(WIDE_TILE // kt):
            s = jnp.zeros((kt, BLK), F32)
            for h in range(IDX_HEADS):
                s = s + jnp.maximum(d[half * kt:(half + 1) * kt, h * BLK:(h + 1) * BLK], 0.0) * wt[h:h + 1, :]
            krow = lax.broadcasted_iota(I32, (kt, BLK), 0) + (r0 + half * kt)
            valid = (krow >= PAD_ROWS) & (krow <= qrow)
            s_ref[pl.ds(r0 + half * kt, kt), :] = jnp.where(valid, s, -jnp.inf)
            lo8 = jnp.minimum(lo8, _fold8(s, jnp.min))
            hi8 = jnp.maximum(hi8, _fold8(s, jnp.max))
        return lo8, hi8

    lo8, hi8 = lax.fori_loop(0, nw, score_tile,
                             (jnp.full((SUBLANES, BLK), jnp.inf, F32), jnp.full((SUBLANES, BLK), -jnp.inf, F32)))

    def count_pass(p):
        def body(t, acc):
            ge = s_ref[rows(t), :] >= p
            return acc + _fold8(jnp.where(ge, 1, 0).astype(I32), functools.partial(jnp.sum, dtype=I32))
        return _colsum(lax.fori_loop(0, nt, body, jnp.zeros((SUBLANES, BLK), I32)))

    def snap_pass(p):
        def body(t, carry):
            acc, up, dn = carry
            s = s_ref[rows(t), :]
            ge = s >= p
            acc = acc + _fold8(jnp.where(ge, 1, 0).astype(I32), functools.partial(jnp.sum, dtype=I32))
            up = jnp.minimum(up, _fold8(jnp.where(ge, s, jnp.inf), jnp.min))
            dn = jnp.maximum(dn, _fold8(jnp.where(ge, -jnp.inf, s), jnp.max))
            return acc, up, dn
        acc, up, dn = lax.fori_loop(0, nt, body, (jnp.zeros((SUBLANES, BLK), I32),
                                                  jnp.full((SUBLANES, BLK), jnp.inf, F32),
                                                  jnp.full((SUBLANES, BLK), -jnp.inf, F32)))
        return _colsum(acc), jnp.min(up, axis=0, keepdims=True), jnp.max(dn, axis=0, keepdims=True)

    def n_active(clo, lo, hi):
        return jnp.max(jnp.where((clo > topk) & (lo < hi), 1, 0).astype(I32))

    lo = jnp.min(lo8, axis=0, keepdims=True)
    hi = jnp.max(hi8, axis=0, keepdims=True)
    clo = lax.broadcasted_iota(I32, (1, BLK), 1) + (q_row0 - PAD_ROWS + 1)
    chi = jnp.zeros((1, BLK), I32)

    def probe(state):
        lo, hi, clo, chi = state
        p = 0.5 * lo + 0.5 * hi
        cnt = count_pass(p)
        act = (clo > topk) & (p > lo) & (p < hi)
        up = act & (cnt >= topk)
        dn = act & (cnt < topk)
        return jnp.where(up, p, lo), jnp.where(dn, p, hi), jnp.where(up, cnt, clo), jnp.where(dn, cnt, chi)

    state = lax.fori_loop(0, BLIND_PROBES, lambda _, st: probe(st), (lo, hi, clo, chi))

    def fast_cond(c):
        return (c[0] < FAST_PROBES) & (c[1] > 0)

    def fast_body(c):
        state = c[2:]
        for _ in range(PROBES_PER_CHECK):
            state = probe(state)
        return (c[0] + PROBES_PER_CHECK, n_active(state[2], state[0], state[1])) + state

    c = lax.while_loop(fast_cond, fast_body, (jnp.int32(BLIND_PROBES), n_active(state[2], state[0], state[1])) + state)
    unsettled = c[1]
    lo, hi, clo, chi = c[2:]

    def plain_bias(thr):
        def body(t, carry):
            bias_ref[rows(t), :] = jnp.where(s_ref[rows(t), :] >= thr, 0.0, NEG_BIG)
            return carry
        lax.fori_loop(0, per_w * nw, body, 0)

    @pl.when(unsettled == 0)
    def _():
        plain_bias(lo)

    @pl.when(unsettled != 0)
    def _():
        def slow_cond(c):
            return n_active(c[2], c[0], c[1]) > 0

        def slow_body(c):
            lo, hi, clo, chi = c
            mid = 0.5 * lo + 0.5 * hi
            p = jnp.where(mid > lo, jnp.minimum(mid, hi), hi)
            cnt, nxt, prv = snap_pass(p)
            act = (clo > topk) & (lo < hi)
            up = act & (cnt >= topk)
            dn = act & (cnt < topk)
            return (jnp.where(up, nxt, lo), jnp.where(dn, prv, hi), jnp.where(up, cnt, clo), jnp.where(dn, cnt, chi))

        thr, _, c_ge, c_gt = lax.while_loop(slow_cond, slow_body, (lo, hi, clo, chi))
        need = (topk - c_gt).astype(F32)
        any_tie = jnp.max(jnp.where(c_ge > topk, 1, 0).astype(I32))

        def tie_bias(t, run):
            s = s_ref[rows(t), :]
            eq = (s == thr) & (c_ge > topk)
            lower = (lax.broadcasted_iota(I32, (kt, kt), 0) >= lax.broadcasted_iota(I32, (kt, kt), 1)).astype(BF16)
            e = jnp.where(eq, 1.0, 0.0)
            rank = jnp.dot(lower, e.astype(BF16), preferred_element_type=F32) + run
            keep = (eq & (rank <= need)) | (jnp.logical_not(eq) & (s >= thr))
            bias_ref[rows(t), :] = jnp.where(keep, 0.0, NEG_BIG)
            return run + jnp.sum(e, axis=0, keepdims=True)

        @pl.when(any_tie == 0)
        def _():
            plain_bias(thr)

        @pl.when(any_tie != 0)
        def _():
            lax.fori_loop(0, per_w * nw, tie_bias, jnp.zeros((1, BLK), F32))

    gw = ATT_GROUP * BLK
    acc_ref[...] = jnp.zeros_like(acc_ref)

    def att_tile(u, m):
        r0 = pl.multiple_of(u * WIDE_TILE, WIDE_TILE)
        b = bias_ref[pl.ds(r0, WIDE_TILE), :]
        b4 = jnp.concatenate([b] * ATT_GROUP, axis=1)
        for g in range(ATT_KV_HEADS):
            qg = jnp.concatenate([qa_ref[0, :, (g * ATT_GROUP + r) * ATT_HD:(g * ATT_GROUP + r + 1) * ATT_HD]
                                  for r in range(ATT_GROUP)], axis=0)
            x_ref[g] = _nt_dot(ka_ref[0, pl.ds(r0, WIDE_TILE), g * ATT_HD:(g + 1) * ATT_HD], qg)
        m_out = []
        for g in range(ATT_KV_HEADS):
            x = x_ref[g] + b4
            m_new = jnp.maximum(m[g], jnp.max(x, axis=0, keepdims=True))
            alpha = jnp.exp2(m[g] - m_new)
            p = jnp.exp2(x - m_new).astype(BF16)
            acc_ref[g] = acc_ref[g] * alpha + jnp.dot(vt_ref[0, u, g], p, preferred_element_type=F32)
            m_out.append(m_new)
        return tuple(m_out)

    lax.fori_loop(0, nw, att_tile, tuple(jnp.full((1, gw), NEG_BIG, F32) for _ in range(ATT_KV_HEADS)))
    for g in range(ATT_KV_HEADS):
        acc = acc_ref[g]
        out_t = acc[:ATT_HD] / acc[ATT_HD:ATT_HD + 1]
        for r in range(ATT_GROUP):
            hh = g * ATT_GROUP + r
            o_ref[0, :, hh * ATT_HD:(hh + 1) * ATT_HD] = out_t[:, r * BLK:(r + 1) * BLK].T.astype(o_ref.dtype)


def _dsa(qa, qi, wt, ka, vt, ki, L, topk):
    B, TK, _ = qa.shape
    nq = L // BLK
    return pl.pallas_call(
        functools.partial(_dsa_kernel, topk=topk),
        grid=(B, nq),
        in_specs=[
            pl.BlockSpec((1, BLK, ATT_Q_W), lambda b, i: (b, i + 1, 0)),
            pl.BlockSpec((1, BLK, IDX_Q_W), lambda b, i: (b, i + 1, 0)),
            pl.BlockSpec((1, IDX_HEADS, BLK), lambda b, i: (b, 0, i + 1)),
            pl.BlockSpec((1, TK, ATT_KV_W), lambda b, i: (b, 0, 0)),
            pl.BlockSpec((1, TK // WIDE_TILE, ATT_KV_HEADS, ATT_HD + SUBLANES, WIDE_TILE), lambda b, i: (b, 0, 0, 0, 0)),
            pl.BlockSpec((1, TK, IDX_HD), lambda b, i: (b, 0, 0)),
        ],
        out_specs=pl.BlockSpec((1, BLK, ATT_Q_W), lambda b, i: (b, i, 0)),
        out_shape=jax.ShapeDtypeStruct((B, L, ATT_Q_W), BF16),
        scratch_shapes=[pltpu.VMEM((TK, BLK), F32), pltpu.VMEM((TK, BLK), F32),
                        pltpu.VMEM((ATT_KV_HEADS, ATT_HD + SUBLANES, ATT_GROUP * BLK), F32),
                        pltpu.VMEM((ATT_KV_HEADS, WIDE_TILE, ATT_GROUP * BLK), F32)],
        compiler_params=_cparams(("parallel", "arbitrary")),
        name="dsa_attn",
    )(qa, qi, wt, ka, vt, ki)


def _merge_kernel(*refs):
    x_ref, yg_ref, yd_ref = refs[:3]
    gate_refs = refs[3:3 + 2 * MERGE_BLOCKS]
    wg_ref, wd_ref, wo_ref, npost_ref, npre_ref, h_ref, n2_ref = refs[3 + 2 * MERGE_BLOCKS:]
    gate = lambda blocks: jnp.concatenate([jax.nn.sigmoid(r[...].astype(F32)) for r in blocks], axis=0)
    bg = jnp.dot(yg_ref[...], wg_ref[...], preferred_element_type=F32)
    bd = jnp.dot(yd_ref[...], wd_ref[...], preferred_element_type=F32)
    merged = gate(gate_refs[:MERGE_BLOCKS]) * bg + gate(gate_refs[MERGE_BLOCKS:]) * bd
    mix = jnp.dot(merged.astype(BF16), wo_ref[...], preferred_element_type=F32)
    h1 = x_ref[...] + _rms(mix, npost_ref[...])
    h_ref[...] = h1
    n2_ref[...] = _rms(h1, npre_ref[...]).astype(BF16)


def _merge(x2, yg, yd, P2, wg, wd, wo, npost, npre, L, TP):
    R = x2.shape[0]
    tm = MERGE_BLOCKS * BLK
    assert L % tm == 0
    per = L // BLK
    def gate_spec(col, k):
        def index(i):
            blk = i * MERGE_BLOCKS + k
            return ((blk // per) * (TP // BLK) + 1 + blk % per, col // D_MODEL)
        return pl.BlockSpec((BLK, D_MODEL), index)
    row = lambda i: (i, 0)
    full = lambda i: (0, 0)
    gates = [gate_spec(C_GA, k) for k in range(MERGE_BLOCKS)] + [gate_spec(C_GB, k) for k in range(MERGE_BLOCKS)]
    return pl.pallas_call(
        _merge_kernel,
        grid=(R // tm,),
        in_specs=[pl.BlockSpec((tm, D_MODEL), row), pl.BlockSpec((tm, GDN_W), row), pl.BlockSpec((tm, ATT_Q_W), row)]
        + gates
        + [pl.BlockSpec((GDN_W, D_MODEL), full), pl.BlockSpec((ATT_Q_W, D_MODEL), full),
           pl.BlockSpec((D_MODEL, D_MODEL), full), pl.BlockSpec((1, D_MODEL), full), pl.BlockSpec((1, D_MODEL), full)],
        out_specs=[pl.BlockSpec((tm, D_MODEL), row), pl.BlockSpec((tm, D_MODEL), row)],
        out_shape=[jax.ShapeDtypeStruct((R, D_MODEL), F32), jax.ShapeDtypeStruct((R, D_MODEL), BF16)],
        compiler_params=_cparams(("parallel",)),
        name="merge_out",
    )(x2, yg, yd, *([P2] * (2 * MERGE_BLOCKS)), wg, wd, wo, npost, npre)


def _mlp_kernel(n_ref, h_ref, wu_ref, wd_ref, g_ref, o_ref, acc_ref):
    j = pl.program_id(1)

    @pl.when(j == 0)
    def _():
        acc_ref[...] = jnp.zeros_like(acc_ref)

    u = jnp.maximum(jnp.dot(n_ref[...], wu_ref[...], preferred_element_type=F32), 0.0)
    acc_ref[...] += jnp.dot((u * u).astype(BF16), wd_ref[...], preferred_element_type=F32)

    @pl.when(j == pl.num_programs(1) - 1)
    def _():
        o_ref[...] = h_ref[...] + _rms(acc_ref[...], g_ref[...])


def _mlp(n2, h1, wu, wd, g, tm, tf):
    R = n2.shape[0]
    return pl.pallas_call(
        _mlp_kernel,
        grid=(R // tm, D_FF // tf),
        in_specs=[
            pl.BlockSpec((tm, D_MODEL), lambda i, j: (i, 0)),
            pl.BlockSpec((tm, D_MODEL), lambda i, j: (i, 0)),
            pl.BlockSpec((D_MODEL, tf), lambda i, j: (0, j)),
            pl.BlockSpec((tf, D_MODEL), lambda i, j: (j, 0)),
            pl.BlockSpec((1, D_MODEL), lambda i, j: (0, 0)),
        ],
        out_specs=pl.BlockSpec((tm, D_MODEL), lambda i, j: (i, 0)),
        out_shape=jax.ShapeDtypeStruct((R, D_MODEL), F32),
        scratch_shapes=[pltpu.VMEM((tm, D_MODEL), F32)],
        compiler_params=_cparams(("parallel", "arbitrary")),
        name="mlp",
    )(n2, h1, wu, wd, g)


def _largest_tile(n, candidates):
    for c in candidates:
        if n % c == 0:
            return c
    raise ValueError(f"no tile for {n}")


def _rope_tables(tp, tk):
    pos = jnp.maximum(jnp.arange(tk, dtype=jnp.int32) - PAD_ROWS, 0)
    pos = jnp.where(jnp.arange(tk) < tp, pos, 0).astype(F32)

    def table(dim):
        inv = ROPE_THETA ** (-jnp.arange(0, dim, 2, dtype=F32) / dim)
        ang = pos[:, None] * inv[None, :]
        c, s = jnp.cos(ang), jnp.sin(ang)
        reps = LANES // dim
        return jnp.tile(jnp.concatenate([c, c], axis=1), (1, reps)), jnp.tile(jnp.concatenate([-s, s], axis=1), (1, reps))

    return table(ATT_HD) + table(IDX_HD)


def kernel(x, meta_tokens, pre_mix_norm, w_in, conv_w, a_log, dt_bias, gdn_norm, w_branch_gdn,
           w_branch_dsa, w_out, post_mix_norm, pre_mlp_norm, w_up, w_down, post_mlp_norm):
    B, L, D = x.shape
    assert D == D_MODEL and L % BLK == 0 and w_in.shape[0] == 1, "single layer, 128-aligned sequence"
    TP = BLK + L
    per_wide = WIDE_TILE // BLK
    nkb = -(-(TP // BLK) // per_wide) * per_wide
    topk = min(TOPK_MAX, L // 4)

    w = w_in[0]
    o = np.cumsum((0, GDN_W, GDN_W, GDN_W, GDN_W, GDN_HEADS, GDN_HEADS, ATT_Q_W, ATT_KV_W, ATT_KV_W,
                   IDX_Q_W, IDX_HD, IDX_HEADS, D_MODEL, D_MODEL))
    seg = lambda i: w[:, o[i]:o[i + 1]]
    w_main = jnp.concatenate([seg(0), seg(1), seg(2), seg(3), seg(6), seg(7), seg(8), seg(9), seg(12), seg(13)],
                             axis=1).astype(BF16)
    w_small = jnp.concatenate([seg(10), seg(11), seg(4), seg(5),
                               jnp.zeros((D, LANES - IDX_HD - IDX_HEADS - 2 * GDN_HEADS), F32)], axis=1)
    lane_vec = lambda v, off: jnp.zeros((1, LANES), F32).at[0, off:off + v.shape[0]].set(v.astype(F32))
    alog_v = lane_vec(a_log[0], S_DECAY)
    dtb_v = lane_vec(dt_bias[0], S_DECAY)

    hp = jnp.concatenate([jnp.zeros((B, PAD_ROWS, D), x.dtype),
                          jnp.broadcast_to(meta_tokens.astype(x.dtype)[None], (B, N_META, D)), x], axis=1)
    R = B * TP
    P2, Ps2 = _proj(hp.reshape(R, D), pre_mix_norm[0][None], w_main, w_small,
                    _largest_tile(R, PROJ_ROW_TILES), PROJ_COL_TILE)
    P = P2.reshape(B, TP, N_MAIN)
    Ps = Ps2.reshape(B, TP, LANES)

    y_gdn = _gdn(P, Ps, conv_w[0], alog_v, dtb_v, gdn_norm[0][None])

    cos_a, sin_a, cos_i, sin_i = _rope_tables(TP, nkb * BLK)
    qa, ka, vt, qi, ki, wt = _dsa_prep(P, Ps, cos_a, sin_a, cos_i, sin_i, nkb)
    y_dsa = _dsa(qa, qi, wt, ka, vt, ki, L, topk)

    Rr = B * L
    h1, n2 = _merge(x.reshape(Rr, D), y_gdn.reshape(Rr, GDN_W), y_dsa.reshape(Rr, ATT_Q_W), P2,
                    w_branch_gdn[0].astype(BF16), w_branch_dsa[0].astype(BF16), w_out[0].astype(BF16),
                    post_mix_norm[0][None], pre_mlp_norm[0][None], L, TP)
    out = _mlp(n2, h1, w_up[0].astype(BF16), w_down[0].astype(BF16), post_mlp_norm[0][None],
               _largest_tile(Rr, MLP_ROW_TILES), MLP_FF_TILE)
    return out.reshape(B, L, D)
```

```python
import functools
import math

import jax
import jax.numpy as jnp
import numpy as np
from jax import lax
from jax.experimental import pallas as pl
from jax.experimental.pallas import tpu as pltpu

F32 = jnp.float32
BF16 = jnp.bfloat16
I32 = jnp.int32

D_MODEL = 1024
N_META = 16
GDN_HEADS = 8
GDN_DK = 128
GDN_DV = 128
CONV_WIDTH = 4
ATT_HEADS = 8
ATT_KV_HEADS = 2
ATT_HD = 128
ATT_GROUP = ATT_HEADS // ATT_KV_HEADS
IDX_HEADS = 8
IDX_HD = 64
TOPK_MAX = 256
ROPE_THETA = 10000.0
D_FF = 4 * D_MODEL
EPS = 1e-6

GDN_W = GDN_HEADS * GDN_DK
ATT_Q_W = ATT_HEADS * ATT_HD
ATT_KV_W = ATT_KV_HEADS * ATT_HD
IDX_Q_W = IDX_HEADS * IDX_HD

LANES = 128
SUBLANES = 8
BLK = 128
PAD_ROWS = BLK - N_META
KEY_TILE = 256
COUNT_TILE = 512
WIDE_TILE = 1024
VMEM_LIMIT = 56 * 1024 * 1024
PROJ_ROW_TILES = (1024, 768, 512, 384, 256, 128)
PROJ_COL_TILE = 2048
MLP_ROW_TILES = (1024, 512, 256, 128)
MLP_FF_TILE = 1024
INV_FULL_ROUNDS = 3
GDN_HEAD_GROUP = 8
MERGE_BLOCKS = 4
PREV_ROWS = 16

C_GQ, C_GK, C_GV, C_GZ = 0, 1024, 2048, 3072
C_AQ, C_AK, C_AV, C_IQ = 4096, 5120, 5376, 5632
C_GA, C_GB = 6144, 7168
N_MAIN = 8192
S_IK, S_IW, S_BETA, S_DECAY = 0, 64, 72, 80

NEG_BIG = -1e30
BLIND_PROBES = 16
PROBES_PER_CHECK = 4
FAST_PROBES = 20


def _cparams(sem):
    return pltpu.CompilerParams(dimension_semantics=sem, vmem_limit_bytes=VMEM_LIMIT)


def _split3(a):
    h = a.astype(BF16)
    r = a - h.astype(F32)
    m = r.astype(BF16)
    l = (r - m.astype(F32)).astype(BF16)
    return h, m, l


def _split2(a):
    h = a.astype(BF16)
    return h, (a - h.astype(F32)).astype(BF16)


def _rms(x, g):
    return x * lax.rsqrt(jnp.mean(x * x, axis=-1, keepdims=True) + EPS) * g


def _nt_dot(a, b):
    return lax.dot_general(a, b, (((1,), (1,)), ((), ())), preferred_element_type=F32)


def _proj_kernel(h_ref, g_ref, w_ref, ws_ref, o_ref, os_ref, n_ref):
    j = pl.program_id(1)

    @pl.when(j == 0)
    def _():
        n = _rms(h_ref[...], g_ref[...])
        n_ref[...] = n.astype(BF16)
        nh, nl = _split2(n)
        wh, wl = _split2(ws_ref[...])
        os_ref[...] = (jnp.dot(nh, wh, preferred_element_type=F32)
                       + jnp.dot(nl, wh, preferred_element_type=F32)
                       + jnp.dot(nh, wl, preferred_element_type=F32))

    o_ref[...] = jnp.dot(n_ref[...], w_ref[...], preferred_element_type=F32).astype(o_ref.dtype)


def _proj(hp, g, w_main, w_small, tm, tn):
    R = hp.shape[0]
    return pl.pallas_call(
        _proj_kernel,
        grid=(R // tm, N_MAIN // tn),
        in_specs=[
            pl.BlockSpec((tm, D_MODEL), lambda i, j: (i, 0)),
            pl.BlockSpec((1, D_MODEL), lambda i, j: (0, 0)),
            pl.BlockSpec((D_MODEL, tn), lambda i, j: (0, j)),
            pl.BlockSpec((D_MODEL, LANES), lambda i, j: (0, 0)),
        ],
        out_specs=[
            pl.BlockSpec((tm, tn), lambda i, j: (i, j)),
            pl.BlockSpec((tm, LANES), lambda i, j: (i, 0)),
        ],
        out_shape=[
            jax.ShapeDtypeStruct((R, N_MAIN), BF16),
            jax.ShapeDtypeStruct((R, LANES), F32),
        ],
        scratch_shapes=[pltpu.VMEM((tm, D_MODEL), BF16)],
        compiler_params=_cparams(("parallel", "arbitrary")),
        name="in_proj",
    )(hp, g, w_main, w_small)


def _mm3(a_parts, b_parts):
    ah, al = a_parts
    bh, bl = b_parts
    lhs = jnp.concatenate([ah, al, ah], axis=1)
    rhs = jnp.concatenate([bh, bh, bl], axis=0)
    return jnp.dot(lhs, rhs, preferred_element_type=F32)


def _unit_upper_inverses(at_list, eye):
    heads = range(len(at_list))
    sp = [_split2(a) for a in at_list]
    q = [_mm3(sp[h], sp[h]) for h in heads]
    y = [eye - a for a in at_list]
    for j in range(5):
        z = [jnp.concatenate([y[h], q[h]], axis=1) for h in heads]
        if j < INV_FULL_ROUNDS:
            zs = [_split2(v) for v in z]
            r = [_mm3((zs[h][0][:, BLK:], zs[h][1][:, BLK:]), zs[h]) for h in heads]
        else:
            qs = [_split2(v) for v in q]
            zb = [v.astype(BF16) for v in z]
            r = [jnp.dot(jnp.concatenate(qs[h], axis=1), jnp.concatenate([zb[h], zb[h]], axis=0),
                         preferred_element_type=F32) for h in heads]
        y = [y[h] + r[h][:, :BLK] for h in heads]
        q = [r[h][:, BLK:] for h in heads]
    qs = [_split2(v) for v in q]
    yb = [v.astype(BF16) for v in y]
    return [y[h] + jnp.dot(jnp.concatenate(qs[h], axis=1), jnp.concatenate([yb[h], yb[h]], axis=0),
                           preferred_element_type=F32) for h in heads]


def _gdn_kernel(x_ref, xp_ref, ps_ref, z_ref, cw_ref, alog_ref, dtb_ref, nw_ref, o_ref, s_ref):
    n = pl.program_id(1)
    heads = range(GDN_HEADS)

    @pl.when(n == 0)
    def _():
        s_ref[...] = jnp.zeros_like(s_ref)

    prev = xp_ref[0]
    x_bf = x_ref[0]
    xb = jnp.concatenate([jnp.where(n == 0, jnp.zeros_like(prev), prev), x_bf], axis=0)
    sr = lax.broadcasted_iota(I32, (BLK, PREV_ROWS + BLK), 0)
    sc = lax.broadcasted_iota(I32, (BLK, PREV_ROWS + BLK), 1)
    y = x_bf.astype(F32) * cw_ref[CONV_WIDTH - 1:CONV_WIDTH, :]
    for t in range(CONV_WIDTH - 1):
        shift = (sc == sr + (PREV_ROWS - (CONV_WIDTH - 1) + t)).astype(BF16)
        y = y + jnp.dot(shift, xb, preferred_element_type=F32) * cw_ref[t:t + 1, :]
    y = y * jax.nn.sigmoid(y)

    s = ps_ref[0]
    row = lax.broadcasted_iota(I32, (BLK, LANES), 0) + n * BLK
    lane = lax.broadcasted_iota(I32, (BLK, LANES), 1)
    xs = s + dtb_ref[...]
    softplus = jnp.maximum(xs, 0.0) + jnp.log(1.0 + jnp.exp(-jnp.abs(xs)))
    gb = jnp.where((lane >= S_DECAY) & (lane < S_DECAY + GDN_HEADS), -jnp.exp(alog_ref[...]) * softplus,
                   jnp.where((lane >= S_BETA) & (lane < S_BETA + GDN_HEADS), jax.nn.sigmoid(s), 0.0))
    gb = jnp.where(row >= PAD_ROWS, gb, 0.0)

    ri = lax.broadcasted_iota(I32, (BLK, BLK), 0)
    ci = lax.broadcasted_iota(I32, (BLK, BLK), 1)
    tri_incl = ri >= ci
    eye = (ri == ci).astype(F32)
    ones_l = tri_incl.astype(BF16)

    gh, gm, gl = _split3(gb)
    gc = jnp.dot(jnp.concatenate([ones_l, ones_l, ones_l], axis=1),
                 jnp.concatenate([gh, gm, gl], axis=0), preferred_element_type=F32)
    sel = (lax.broadcasted_iota(I32, (16, LANES), 1)
           == lax.broadcasted_iota(I32, (16, LANES), 0) + S_DECAY).astype(BF16)
    ch, cm, cl = _split3(gc)
    gct = _nt_dot(jnp.concatenate([sel, sel, sel], axis=1), jnp.concatenate([ch, cm, cl], axis=1))

    def head_cols(base, h):
        return y[:, base + h * GDN_DK:base + (h + 1) * GDN_DK]

    ones_sq = jnp.ones((GDN_DK, GDN_DK), BF16)

    def row_sums(sq):
        hi, lo = _split2(sq)
        return jnp.dot(jnp.concatenate([hi, lo], axis=1), jnp.concatenate([ones_sq, ones_sq], axis=0),
                       preferred_element_type=F32)

    def head_group(hs):
        q = {h: head_cols(C_GQ, h) for h in hs}
        k = {h: head_cols(C_GK, h) for h in hs}
        v = {h: head_cols(C_GV, h) for h in hs}
        q = {h: a * lax.rsqrt(row_sums(a * a) + EPS) * (GDN_DK ** -0.5) for h, a in q.items()}
        k = {h: a * lax.rsqrt(row_sums(a * a) + EPS) for h, a in k.items()}
        beta = {h: gb[:, S_BETA + h:S_BETA + h + 1] for h in hs}
        gcc = {h: gc[:, S_DECAY + h:S_DECAY + h + 1] for h in hs}
        g_last = {h: gcc[h][BLK - 1:BLK, :] for h in hs}
        decay = {h: jnp.where(tri_incl, jnp.exp(jnp.minimum(gcc[h] - gct[h:h + 1, :], 0.0)), 0.0) for h in hs}
        e_gc = {h: jnp.exp(gcc[h]) for h in hs}
        kb = {h: k[h] * beta[h] for h in hs}
        kbf = {h: k[h].astype(BF16) for h in hs}
        decay_t = {h: jnp.exp(jnp.minimum(gct[h:h + 1, :] - gcc[h], 0.0)) for h in hs}
        kk_t = {h: _nt_dot(kbf[h], kb[h].astype(BF16)) for h in hs}
        qk = {h: _nt_dot(q[h].astype(BF16), kbf[h]) for h in hs}
        tm_t = _unit_upper_inverses([jnp.where(ri < ci, kk_t[h] * decay_t[h], 0.0) for h in hs], eye)
        tm = {h: t.T for h, t in zip(hs, tm_t)}
        uw = {h: jnp.dot(tm[h].astype(BF16),
                         jnp.concatenate([v[h] * beta[h], kb[h] * e_gc[h]], axis=1).astype(BF16),
                         preferred_element_type=F32) for h in hs}
        attn = {h: (qk[h] * decay[h]).astype(BF16) for h in hs}
        q_dec = {h: (q[h] * e_gc[h]).astype(BF16) for h in hs}
        k_tail = {h: (k[h] * jnp.exp(g_last[h] - gcc[h])).T.astype(BF16) for h in hs}

        s_old = {h: s_ref[h] for h in hs}
        s_bf = {h: s_old[h].astype(BF16) for h in hs}
        v_new = {h: uw[h][:, :GDN_DV] - jnp.dot(uw[h][:, GDN_DV:].astype(BF16), s_bf[h], preferred_element_type=F32)
                 for h in hs}
        vn_bf = {h: v_new[h].astype(BF16) for h in hs}
        o = {h: jnp.dot(q_dec[h], s_bf[h], preferred_element_type=F32)
             + jnp.dot(attn[h], vn_bf[h], preferred_element_type=F32) for h in hs}
        for h in hs:
            s_ref[h] = s_old[h] * jnp.exp(g_last[h]) + jnp.dot(k_tail[h], vn_bf[h], preferred_element_type=F32)
        for h in hs:
            z = z_ref[0, :, h * GDN_DV:(h + 1) * GDN_DV].astype(F32)
            o_ref[0, :, h * GDN_DV:(h + 1) * GDN_DV] = (_rms(o[h], nw_ref[...]) * (z * jax.nn.sigmoid(z))).astype(o_ref.dtype)

    for g0 in range(0, GDN_HEADS, GDN_HEAD_GROUP):
        head_group(list(range(g0, g0 + GDN_HEAD_GROUP)))


def _gdn(P, Ps, conv_w, alog_v, dtb_v, norm_w):
    B, TP, _ = P.shape
    nb = TP // BLK
    L = TP - BLK
    wq = 3 * GDN_W
    const = lambda b, n: (0, 0)
    return pl.pallas_call(
        _gdn_kernel,
        grid=(B, nb),
        in_specs=[
            pl.BlockSpec((1, BLK, wq), lambda b, n: (b, n, 0)),
            pl.BlockSpec((1, PREV_ROWS, wq), lambda b, n: (b, jnp.maximum(n * (BLK // PREV_ROWS) - 1, 0), 0)),
            pl.BlockSpec((1, BLK, LANES), lambda b, n: (b, n, 0)),
            pl.BlockSpec((1, BLK, GDN_W), lambda b, n: (b, n, C_GZ // GDN_W)),
            pl.BlockSpec((CONV_WIDTH, wq), const),
            pl.BlockSpec((1, LANES), const),
            pl.BlockSpec((1, LANES), const),
            pl.BlockSpec((1, GDN_DV), const),
        ],
        out_specs=pl.BlockSpec((1, BLK, GDN_W), lambda b, n: (b, jnp.maximum(n - 1, 0), 0)),
        out_shape=jax.ShapeDtypeStruct((B, L, GDN_W), BF16),
        scratch_shapes=[pltpu.VMEM((GDN_HEADS, GDN_DK, GDN_DV), F32)],
        compiler_params=_cparams(("parallel", "arbitrary")),
        name="gdn_scan",
    )(P, P, Ps, P, conv_w, alog_v, dtb_v, norm_w)


def _rope128(x, cos, sin):
    return x * cos + pltpu.roll(x, ATT_HD // 2, 1) * sin


def _rope64(x, cos, sin, first_half):
    rot = jnp.where(first_half, pltpu.roll(x, LANES - IDX_HD // 2, 1), pltpu.roll(x, IDX_HD // 2, 1))
    return x * cos + rot * sin


def _dsa_prep_kernel(aq_ref, akv_ref, iq_ref, s_ref, ca_ref, sa_ref, ci_ref, si_ref,
                     qa_ref, ka_ref, vt_ref, qi_ref, ki_ref, wt_ref, *, n_real, q_scale):
    n = pl.program_id(1)
    live = n < n_real
    ca, sa, ci, si = ca_ref[...], sa_ref[...], ci_ref[...], si_ref[...]
    lane = lax.broadcasted_iota(I32, (BLK, LANES), 1)
    first_half = (lane % IDX_HD) < (IDX_HD // 2)
    zero = jnp.zeros((BLK, LANES), F32)

    for h in range(ATT_HEADS):
        sl = slice(h * ATT_HD, (h + 1) * ATT_HD)
        qa_ref[0, :, sl] = (_rope128(aq_ref[0, :, sl].astype(F32), ca, sa) * q_scale).astype(BF16)
    for g in range(ATT_KV_HEADS):
        sl = slice(g * ATT_HD, (g + 1) * ATT_HD)
        k = _rope128(akv_ref[0, :, sl].astype(F32), ca, sa)
        ka_ref[0, :, sl] = jnp.where(live, k, zero).astype(BF16)
        v = akv_ref[0, :, ATT_KV_W + g * ATT_HD:ATT_KV_W + (g + 1) * ATT_HD].astype(F32)
        vt_ref[0, 0, g, :ATT_HD, :] = jnp.where(live, v, zero).T.astype(BF16)
        vt_ref[0, 0, g, ATT_HD:, :] = jnp.ones((SUBLANES, BLK), BF16)
    for p in range(IDX_Q_W // LANES):
        sl = slice(p * LANES, (p + 1) * LANES)
        qi_ref[0, :, sl] = _rope64(iq_ref[0, :, sl].astype(F32), ci, si, first_half).astype(BF16)
    s = s_ref[0]
    ki = _rope64(s, ci, si, first_half)
    ki_ref[0] = jnp.where(live, ki, zero)[:, :IDX_HD].astype(BF16)
    wsc = s * ((IDX_HEADS * IDX_HD) ** -0.5)
    wh, wm, wl = _split3(wsc)
    sel = (lax.broadcasted_iota(I32, (16, LANES), 1)
           == lax.broadcasted_iota(I32, (16, LANES), 0) + S_IW).astype(BF16)
    wt = _nt_dot(jnp.concatenate([sel, sel, sel], axis=1), jnp.concatenate([wh, wm, wl], axis=1))
    wt_ref[0] = wt[:IDX_HEADS]


def _dsa_prep(P, Ps, cos_a, sin_a, cos_i, sin_i, nkb):
    B, TP, _ = P.shape
    n_real = TP // BLK
    TK = nkb * BLK
    cl = lambda b, n: (b, jnp.minimum(n, n_real - 1), 0)
    tab = lambda b, n: (jnp.minimum(n, n_real - 1), 0)
    return pl.pallas_call(
        functools.partial(_dsa_prep_kernel, n_real=n_real, q_scale=ATT_HD ** -0.5 * math.log2(math.e)),
        grid=(B, nkb),
        in_specs=[
            pl.BlockSpec((1, BLK, ATT_Q_W), lambda b, n: (b, jnp.minimum(n, n_real - 1), C_AQ // ATT_Q_W)),
            pl.BlockSpec((1, BLK, 2 * ATT_KV_W), lambda b, n: (b, jnp.minimum(n, n_real - 1), C_AK // (2 * ATT_KV_W))),
            pl.BlockSpec((1, BLK, IDX_Q_W), lambda b, n: (b, jnp.minimum(n, n_real - 1), C_IQ // IDX_Q_W)),
            pl.BlockSpec((1, BLK, LANES), cl),
            pl.BlockSpec((BLK, LANES), tab),
            pl.BlockSpec((BLK, LANES), tab),
            pl.BlockSpec((BLK, LANES), tab),
            pl.BlockSpec((BLK, LANES), tab),
        ],
        out_specs=[
            pl.BlockSpec((1, BLK, ATT_Q_W), lambda b, n: (b, n, 0)),
            pl.BlockSpec((1, BLK, ATT_KV_W), lambda b, n: (b, n, 0)),
            pl.BlockSpec((1, 1, ATT_KV_HEADS, ATT_HD + SUBLANES, BLK),
                         lambda b, n: (b, n // (WIDE_TILE // BLK), 0, 0, n % (WIDE_TILE // BLK))),
            pl.BlockSpec((1, BLK, IDX_Q_W), lambda b, n: (b, n, 0)),
            pl.BlockSpec((1, BLK, IDX_HD), lambda b, n: (b, n, 0)),
            pl.BlockSpec((1, IDX_HEADS, BLK), lambda b, n: (b, 0, n)),
        ],
        out_shape=[
            jax.ShapeDtypeStruct((B, TK, ATT_Q_W), BF16),
            jax.ShapeDtypeStruct((B, TK, ATT_KV_W), BF16),
            jax.ShapeDtypeStruct((B, TK // WIDE_TILE, ATT_KV_HEADS, ATT_HD + SUBLANES, WIDE_TILE), BF16),
            jax.ShapeDtypeStruct((B, TK, IDX_Q_W), BF16),
            jax.ShapeDtypeStruct((B, TK, IDX_HD), BF16),
            jax.ShapeDtypeStruct((B, IDX_HEADS, TK), F32),
        ],
        compiler_params=_cparams(("parallel", "arbitrary")),
        name="dsa_prep",
    )(P, P, P, Ps, cos_a, sin_a, cos_i, sin_i)


def _colsum(x):
    return jnp.sum(x, axis=0, keepdims=True, dtype=x.dtype)


def _fold8(x, op):
    return op(x.reshape(x.shape[0] // SUBLANES, SUBLANES, BLK), axis=0)


def _dsa_kernel(qa_ref, qi_ref, wt_ref, ka_ref, vt_ref, ki_ref, o_ref,
                s_ref, bias_ref, acc_ref, x_ref, *, topk):
    i = pl.program_id(1)
    q_row0 = (i + 1) * BLK
    nt = (i + 3) // 2
    kt = KEY_TILE
    qrow = lax.broadcasted_iota(I32, (kt, BLK), 1) + q_row0

    def rows(t):
        return pl.ds(pl.multiple_of(t * kt, kt), kt)

    qi = qi_ref[0]
    qs = jnp.concatenate([qi[:, h * IDX_HD:(h + 1) * IDX_HD] for h in range(IDX_HEADS)], axis=0)
    wt = wt_ref[0]
    per_w = WIDE_TILE // kt
    nw = (nt + per_w - 1) // per_w

    def score_tile(u, carry):
        lo8, hi8 = carry
        r0 = pl.multiple_of(u * WIDE_TILE, WIDE_TILE)
        d = _nt_dot(ki_ref[0, pl.ds(r0, WIDE_TILE), :], qs)
        for half in range(WIDE_TILE // kt):
            s = jnp.zeros((kt, BLK), F32)
            for h in range(IDX_HEADS):
                s = s + jnp.maximum(d[half * kt:(half + 1) * kt, h * BLK:(h + 1) * BLK], 0.0) * wt[h:h + 1, :]
            krow = lax.broadcasted_iota(I32, (kt, BLK), 0) + (r0 + half * kt)
            valid = (krow >= PAD_ROWS) & (krow <= qrow)
            s_ref[pl.ds(r0 + half * kt, kt), :] = jnp.where(valid, s, -jnp.inf)
            lo8 = jnp.minimum(lo8, _fold8(s, jnp.min))
            hi8 = jnp.maximum(hi8, _fold8(s, jnp.max))
        return lo8, hi8

    lo8, hi8 = lax.fori_loop(0, nw, score_tile,
                             (jnp.full((SUBLANES, BLK), jnp.inf, F32), jnp.full((SUBLANES, BLK), -jnp.inf, F32)))

    def count_pass(p):
        def body(t, acc):
            ge = s_ref[pl.ds(pl.multiple_of(t * COUNT_TILE, COUNT_TILE), COUNT_TILE), :] >= p
            return acc + _fold8(jnp.where(ge, 1, 0).astype(I32), functools.partial(jnp.sum, dtype=I32))
        n_count = (nt * kt + COUNT_TILE - 1) // COUNT_TILE
        return _colsum(lax.fori_loop(0, n_count, body, jnp.zeros((SUBLANES, BLK), I32)))

    def snap_pass(p):
        def body(t, carry):
            acc, up, dn = carry
            s = s_ref[rows(t), :]
            ge = s >= p
            acc = acc + _fold8(jnp.where(ge, 1, 0).astype(I32), functools.partial(jnp.sum, dtype=I32))
            up = jnp.minimum(up, _fold8(jnp.where(ge, s, jnp.inf), jnp.min))
            dn = jnp.maximum(dn, _fold8(jnp.where(ge, -jnp.inf, s), jnp.max))
            return acc, up, dn
        acc, up, dn = lax.fori_loop(0, nt, body, (jnp.zeros((SUBLANES, BLK), I32),
                                                  jnp.full((SUBLANES, BLK), jnp.inf, F32),
                                                  jnp.full((SUBLANES, BLK), -jnp.inf, F32)))
        return _colsum(acc), jnp.min(up, axis=0, keepdims=True), jnp.max(dn, axis=0, keepdims=True)

    def n_active(clo, lo, hi):
        return jnp.max(jnp.where((clo > topk) & (lo < hi), 1, 0).astype(I32))

    lo = jnp.min(lo8, axis=0, keepdims=True)
    hi = jnp.max(hi8, axis=0, keepdims=True)
    clo = lax.broadcasted_iota(I32, (1, BLK), 1) + (q_row0 - PAD_ROWS + 1)
    chi = jnp.zeros((1, BLK), I32)

    def probe(state):
        lo, hi, clo, chi = state
        p = 0.5 * lo + 0.5 * hi
        cnt = count_pass(p)
        act = (clo > topk) & (p > lo) & (p < hi)
        up = act & (cnt >= topk)
        dn = act & (cnt < topk)
        return jnp.where(up, p, lo), jnp.where(dn, p, hi), jnp.where(up, cnt, clo), jnp.where(dn, cnt, chi)

    state = lax.fori_loop(0, BLIND_PROBES, lambda _, st: probe(st), (lo, hi, clo, chi))

    def fast_cond(c):
        return (c[0] < FAST_PROBES) & (c[1] > 0)

    def fast_body(c):
        state = c[2:]
        for _ in range(PROBES_PER_CHECK):
            state = probe(state)
        return (c[0] + PROBES_PER_CHECK, n_active(state[2], state[0], state[1])) + state

    c = lax.while_loop(fast_cond, fast_body, (jnp.int32(BLIND_PROBES), n_active(state[2], state[0], state[1])) + state)
    unsettled = c[1]
    lo, hi, clo, chi = c[2:]

    def plain_bias(thr):
        def body(t, carry):
            bias_ref[rows(t), :] = jnp.where(s_ref[rows(t), :] >= thr, 0.0, NEG_BIG)
            return carry
        lax.fori_loop(0, per_w * nw, body, 0)

    @pl.when(unsettled == 0)
    def _():
        plain_bias(lo)

    @pl.when(unsettled != 0)
    def _():
        def slow_cond(c):
            return n_active(c[2], c[0], c[1]) > 0

        def slow_body(c):
            lo, hi, clo, chi = c
            mid = 0.5 * lo + 0.5 * hi
            p = jnp.where(mid > lo, jnp.minimum(mid, hi), hi)
            cnt, nxt, prv = snap_pass(p)
            act = (clo > topk) & (lo < hi)
            up = act & (cnt >= topk)
            dn = act & (cnt < topk)
            return (jnp.where(up, nxt, lo), jnp.where(dn, prv, hi), jnp.where(up, cnt, clo), jnp.where(dn, cnt, chi))

        thr, _, c_ge, c_gt = lax.while_loop(slow_cond, slow_body, (lo, hi, clo, chi))
        need = (topk - c_gt).astype(F32)
        any_tie = jnp.max(jnp.where(c_ge > topk, 1, 0).astype(I32))

        def tie_bias(t, run):
            s = s_ref[rows(t), :]
            eq = (s == thr) & (c_ge > topk)
            lower = (lax.broadcasted_iota(I32, (kt, kt), 0) >= lax.broadcasted_iota(I32, (kt, kt), 1)).astype(BF16)
            e = jnp.where(eq, 1.0, 0.0)
            rank = jnp.dot(lower, e.astype(BF16), preferred_element_type=F32) + run
            keep = (eq & (rank <= need)) | (jnp.logical_not(eq) & (s >= thr))
            bias_ref[rows(t), :] = jnp.where(keep, 0.0, NEG_BIG)
            return run + jnp.sum(e, axis=0, keepdims=True)

        @pl.when(any_tie == 0)
        def _():
            plain_bias(thr)

        @pl.when(any_tie != 0)
        def _():
            lax.fori_loop(0, per_w * nw, tie_bias, jnp.zeros((1, BLK), F32))

    gw = ATT_GROUP * BLK
    acc_ref[...] = jnp.zeros_like(acc_ref)

    def att_tile(u, m):
        r0 = pl.multiple_of(u * WIDE_TILE, WIDE_TILE)
        b = bias_ref[pl.ds(r0, WIDE_TILE), :]
        b4 = jnp.concatenate([b] * ATT_GROUP, axis=1)
        for g in range(ATT_KV_HEADS):
            qg = jnp.concatenate([qa_ref[0, :, (g * ATT_GROUP + r) * ATT_HD:(g * ATT_GROUP + r + 1) * ATT_HD]
                                  for r in range(ATT_GROUP)], axis=0)
            x_ref[g] = _nt_dot(ka_ref[0, pl.ds(r0, WIDE_TILE), g * ATT_HD:(g + 1) * ATT_HD], qg)
        m_out = []
        for g in range(ATT_KV_HEADS):
            x = x_ref[g] + b4
            m_new = jnp.maximum(m[g], jnp.max(x, axis=0, keepdims=True))
            alpha = jnp.exp2(m[g] - m_new)
            p = jnp.exp2(x - m_new).astype(BF16)
            acc_ref[g] = acc_ref[g] * alpha + jnp.dot(vt_ref[0, u, g], p, preferred_element_type=F32)
            m_out.append(m_new)
        return tuple(m_out)

    lax.fori_loop(0, nw, att_tile, tuple(jnp.full((1, gw), NEG_BIG, F32) for _ in range(ATT_KV_HEADS)))
    for g in range(ATT_KV_HEADS):
        acc = acc_ref[g]
        out_t = acc[:ATT_HD] / acc[ATT_HD:ATT_HD + 1]
        for r in range(ATT_GROUP):
            hh = g * ATT_GROUP + r
            o_ref[0, :, hh * ATT_HD:(hh + 1) * ATT_HD] = out_t[:, r * BLK:(r + 1) * BLK].T.astype(o_ref.dtype)


def _dsa(qa, qi, wt, ka, vt, ki, L, topk):
    B, TK, _ = qa.shape
    nq = L // BLK
    return pl.pallas_call(
        functools.partial(_dsa_kernel, topk=topk),
        grid=(B, nq),
        in_specs=[
            pl.BlockSpec((1, BLK, ATT_Q_W), lambda b, i: (b, i + 1, 0)),
            pl.BlockSpec((1, BLK, IDX_Q_W), lambda b, i: (b, i + 1, 0)),
            pl.BlockSpec((1, IDX_HEADS, BLK), lambda b, i: (b, 0, i + 1)),
            pl.BlockSpec((1, TK, ATT_KV_W), lambda b, i: (b, 0, 0)),
            pl.BlockSpec((1, TK // WIDE_TILE, ATT_KV_HEADS, ATT_HD + SUBLANES, WIDE_TILE), lambda b, i: (b, 0, 0, 0, 0)),
            pl.BlockSpec((1, TK, IDX_HD), lambda b, i: (b, 0, 0)),
        ],
        out_specs=pl.BlockSpec((1, BLK, ATT_Q_W), lambda b, i: (b, i, 0)),
        out_shape=jax.ShapeDtypeStruct((B, L, ATT_Q_W), BF16),
        scratch_shapes=[pltpu.VMEM((TK, BLK), F32), pltpu.VMEM((TK, BLK), F32),
                        pltpu.VMEM((ATT_KV_HEADS, ATT_HD + SUBLANES, ATT_GROUP * BLK), F32),
                        pltpu.VMEM((ATT_KV_HEADS, WIDE_TILE, ATT_GROUP * BLK), F32)],
        compiler_params=_cparams(("parallel", "arbitrary")),
        name="dsa_attn",
    )(qa, qi, wt, ka, vt, ki)


def _merge_kernel(*refs):
    x_ref, yg_ref, yd_ref = refs[:3]
    gate_refs = refs[3:3 + 2 * MERGE_BLOCKS]
    wg_ref, wd_ref, wo_ref, npost_ref, npre_ref, h_ref, n2_ref = refs[3 + 2 * MERGE_BLOCKS:]
    gate = lambda blocks: jnp.concatenate([jax.nn.sigmoid(r[...].astype(F32)) for r in blocks], axis=0)
    bg = jnp.dot(yg_ref[...], wg_ref[...], preferred_element_type=F32)
    bd = jnp.dot(yd_ref[...], wd_ref[...], preferred_element_type=F32)
    merged = gate(gate_refs[:MERGE_BLOCKS]) * bg + gate(gate_refs[MERGE_BLOCKS:]) * bd
    mix = jnp.dot(merged.astype(BF16), wo_ref[...], preferred_element_type=F32)
    h1 = x_ref[...] + _rms(mix, npost_ref[...])
    h_ref[...] = h1
    n2_ref[...] = _rms(h1, npre_ref[...]).astype(BF16)


def _merge(x2, yg, yd, P2, wg, wd, wo, npost, npre, L, TP):
    R = x2.shape[0]
    tm = MERGE_BLOCKS * BLK
    assert L % tm == 0
    per = L // BLK
    def gate_spec(col, k):
        def index(i):
            blk = i * MERGE_BLOCKS + k
            return ((blk // per) * (TP // BLK) + 1 + blk % per, col // D_MODEL)
        return pl.BlockSpec((BLK, D_MODEL), index)
    row = lambda i: (i, 0)
    full = lambda i: (0, 0)
    gates = [gate_spec(C_GA, k) for k in range(MERGE_BLOCKS)] + [gate_spec(C_GB, k) for k in range(MERGE_BLOCKS)]
    return pl.pallas_call(
        _merge_kernel,
        grid=(R // tm,),
        in_specs=[pl.BlockSpec((tm, D_MODEL), row), pl.BlockSpec((tm, GDN_W), row), pl.BlockSpec((tm, ATT_Q_W), row)]
        + gates
        + [pl.BlockSpec((GDN_W, D_MODEL), full), pl.BlockSpec((ATT_Q_W, D_MODEL), full),
           pl.BlockSpec((D_MODEL, D_MODEL), full), pl.BlockSpec((1, D_MODEL), full), pl.BlockSpec((1, D_MODEL), full)],
        out_specs=[pl.BlockSpec((tm, D_MODEL), row), pl.BlockSpec((tm, D_MODEL), row)],
        out_shape=[jax.ShapeDtypeStruct((R, D_MODEL), F32), jax.ShapeDtypeStruct((R, D_MODEL), BF16)],
        compiler_params=_cparams(("parallel",)),
        name="merge_out",
    )(x2, yg, yd, *([P2] * (2 * MERGE_BLOCKS)), wg, wd, wo, npost, npre)


def _mlp_kernel(n_ref, h_ref, wu_ref, wd_ref, g_ref, o_ref, acc_ref):
    j = pl.program_id(1)

    @pl.when(j == 0)
    def _():
        acc_ref[...] = jnp.zeros_like(acc_ref)

    u = jnp.maximum(jnp.dot(n_ref[...], wu_ref[...], preferred_element_type=F32), 0.0)
    acc_ref[...] += jnp.dot((u * u).astype(BF16), wd_ref[...], preferred_element_type=F32)

    @pl.when(j == pl.num_programs(1) - 1)
    def _():
        o_ref[...] = h_ref[...] + _rms(acc_ref[...], g_ref[...])


def _mlp(n2, h1, wu, wd, g, tm, tf):
    R = n2.shape[0]
    return pl.pallas_call(
        _mlp_kernel,
        grid=(R // tm, D_FF // tf),
        in_specs=[
            pl.BlockSpec((tm, D_MODEL), lambda i, j: (i, 0)),
            pl.BlockSpec((tm, D_MODEL), lambda i, j: (i, 0)),
            pl.BlockSpec((D_MODEL, tf), lambda i, j: (0, j)),
            pl.BlockSpec((tf, D_MODEL), lambda i, j: (j, 0)),
            pl.BlockSpec((1, D_MODEL), lambda i, j: (0, 0)),
        ],
        out_specs=pl.BlockSpec((tm, D_MODEL), lambda i, j: (i, 0)),
        out_shape=jax.ShapeDtypeStruct((R, D_MODEL), F32),
        scratch_shapes=[pltpu.VMEM((tm, D_MODEL), F32)],
        compiler_params=_cparams(("parallel", "arbitrary")),
        name="mlp",
    )(n2, h1, wu, wd, g)


def _largest_tile(n, candidates):
    for c in candidates:
        if n % c == 0:
            return c
    raise ValueError(f"no tile for {n}")


def _rope_tables(tp, tk):
    pos = jnp.maximum(jnp.arange(tk, dtype=jnp.int32) - PAD_ROWS, 0)
    pos = jnp.where(jnp.arange(tk) < tp, pos, 0).astype(F32)

    def table(dim):
        inv = ROPE_THETA ** (-jnp.arange(0, dim, 2, dtype=F32) / dim)
        ang = pos[:, None] * inv[None, :]
        c, s = jnp.cos(ang), jnp.sin(ang)
        reps = LANES // dim
        return jnp.tile(jnp.concatenate([c, c], axis=1), (1, reps)), jnp.tile(jnp.concatenate([-s, s], axis=1), (1, reps))

    return table(ATT_HD) + table(IDX_HD)


def kernel(x, meta_tokens, pre_mix_norm, w_in, conv_w, a_log, dt_bias, gdn_norm, w_branch_gdn,
           w_branch_dsa, w_out, post_mix_norm, pre_mlp_norm, w_up, w_down, post_mlp_norm):
    B, L, D = x.shape
    assert D == D_MODEL and L % BLK == 0 and w_in.shape[0] == 1, "single layer, 128-aligned sequence"
    TP = BLK + L
    per_wide = WIDE_TILE // BLK
    nkb = -(-(TP // BLK) // per_wide) * per_wide
    topk = min(TOPK_MAX, L // 4)

    w = w_in[0]
    o = np.cumsum((0, GDN_W, GDN_W, GDN_W, GDN_W, GDN_HEADS, GDN_HEADS, ATT_Q_W, ATT_KV_W, ATT_KV_W,
                   IDX_Q_W, IDX_HD, IDX_HEADS, D_MODEL, D_MODEL))
    seg = lambda i: w[:, o[i]:o[i + 1]]
    w_main = jnp.concatenate([seg(0), seg(1), seg(2), seg(3), seg(6), seg(7), seg(8), seg(9), seg(12), seg(13)],
                             axis=1).astype(BF16)
    w_small = jnp.concatenate([seg(10), seg(11), seg(4), seg(5),
                               jnp.zeros((D, LANES - IDX_HD - IDX_HEADS - 2 * GDN_HEADS), F32)], axis=1)
    lane_vec = lambda v, off: jnp.zeros((1, LANES), F32).at[0, off:off + v.shape[0]].set(v.astype(F32))
    alog_v = lane_vec(a_log[0], S_DECAY)
    dtb_v = lane_vec(dt_bias[0], S_DECAY)

    hp = jnp.concatenate([jnp.zeros((B, PAD_ROWS, D), x.dtype),
                          jnp.broadcast_to(meta_tokens.astype(x.dtype)[None], (B, N_META, D)), x], axis=1)
    R = B * TP
    P2, Ps2 = _proj(hp.reshape(R, D), pre_mix_norm[0][None], w_main, w_small,
                    _largest_tile(R, PROJ_ROW_TILES), PROJ_COL_TILE)
    P = P2.reshape(B, TP, N_MAIN)
    Ps = Ps2.reshape(B, TP, LANES)

    y_gdn = _gdn(P, Ps, conv_w[0], alog_v, dtb_v, gdn_norm[0][None])

    cos_a, sin_a, cos_i, sin_i = _rope_tables(TP, nkb * BLK)
    qa, ka, vt, qi, ki, wt = _dsa_prep(P, Ps, cos_a, sin_a, cos_i, sin_i, nkb)
    y_dsa = _dsa(qa, qi, wt, ka, vt, ki, L, topk)

    Rr = B * L
    h1, n2 = _merge(x.reshape(Rr, D), y_gdn.reshape(Rr, GDN_W), y_dsa.reshape(Rr, ATT_Q_W), P2,
                    w_branch_gdn[0].astype(BF16), w_branch_dsa[0].astype(BF16), w_out[0].astype(BF16),
                    post_mix_norm[0][None], pre_mlp_norm[0][None], L, TP)
    out = _mlp(n2, h1, w_up[0].astype(BF16), w_down[0].astype(BF16), post_mlp_norm[0][None],
               _largest_tile(Rr, MLP_ROW_TILES), MLP_FF_TILE)
    return out.reshape(B, L, D)
```

```python
import functools
import math

import jax
import jax.numpy as jnp
import numpy as np
from jax import lax
from jax.experimental import pallas as pl
from jax.experimental.pallas import tpu as pltpu

F32 = jnp.float32
BF16 = jnp.bfloat16
I32 = jnp.int32

D_MODEL = 1024
N_META = 16
GDN_HEADS = 8
GDN_DK = 128
GDN_DV = 128
CONV_WIDTH = 4
ATT_HEADS = 8
ATT_KV_HEADS = 2
ATT_HD = 128
ATT_GROUP = ATT_HEADS // ATT_KV_HEADS
IDX_HEADS = 8
IDX_HD = 64
TOPK_MAX = 256
ROPE_THETA = 10000.0
D_FF = 4 * D_MODEL
EPS = 1e-6

GDN_W = GDN_HEADS * GDN_DK
ATT_Q_W = ATT_HEADS * ATT_HD
ATT_KV_W = ATT_KV_HEADS * ATT_HD
IDX_Q_W = IDX_HEADS * IDX_HD

LANES = 128
SUBLANES = 8
BLK = 128
PAD_ROWS = BLK - N_META
KEY_TILE = 256
COUNT_TILE = 512
WIDE_TILE = 1024
VMEM_LIMIT = 56 * 1024 * 1024
PROJ_ROW_TILES = (1024, 768, 512, 384, 256, 128)
PROJ_COL_TILE = 2048
MLP_ROW_TILES = (1024, 512, 256, 128)
MLP_FF_TILE = 1024
INV_FULL_ROUNDS = 3
GDN_HEAD_GROUP = 8
MERGE_BLOCKS = 4
PREV_ROWS = 16

C_GQ, C_GK, C_GV, C_GZ = 0, 1024, 2048, 3072
C_AQ, C_AK, C_AV, C_IQ = 4096, 5120, 5376, 5632
C_GA, C_GB = 6144, 7168
N_MAIN = 8192
S_IK, S_IW, S_BETA, S_DECAY = 0, 64, 72, 80

NEG_BIG = -1e30
FAST_PROBES = 20


def _cparams(sem):
    return pltpu.CompilerParams(dimension_semantics=sem, vmem_limit_bytes=VMEM_LIMIT)


def _split3(a):
    h = a.astype(BF16)
    r = a - h.astype(F32)
    m = r.astype(BF16)
    l = (r - m.astype(F32)).astype(BF16)
    return h, m, l


def _split2(a):
    h = a.astype(BF16)
    return h, (a - h.astype(F32)).astype(BF16)


def _rms(x, g):
    return x * lax.rsqrt(jnp.mean(x * x, axis=-1, keepdims=True) + EPS) * g


def _nt_dot(a, b):
    return lax.dot_general(a, b, (((1,), (1,)), ((), ())), preferred_element_type=F32)


def _proj_kernel(h_ref, g_ref, w_ref, ws_ref, o_ref, os_ref, n_ref):
    j = pl.program_id(1)

    @pl.when(j == 0)
    def _():
        n = _rms(h_ref[...], g_ref[...])
        n_ref[...] = n.astype(BF16)
        nh, nl = _split2(n)
        wh, wl = _split2(ws_ref[...])
        os_ref[...] = (jnp.dot(nh, wh, preferred_element_type=F32)
                       + jnp.dot(nl, wh, preferred_element_type=F32)
                       + jnp.dot(nh, wl, preferred_element_type=F32))

    o_ref[...] = jnp.dot(n_ref[...], w_ref[...], preferred_element_type=F32).astype(o_ref.dtype)


def _proj(hp, g, w_main, w_small, tm, tn):
    R = hp.shape[0]
    return pl.pallas_call(
        _proj_kernel,
        grid=(R // tm, N_MAIN // tn),
        in_specs=[
            pl.BlockSpec((tm, D_MODEL), lambda i, j: (i, 0)),
            pl.BlockSpec((1, D_MODEL), lambda i, j: (0, 0)),
            pl.BlockSpec((D_MODEL, tn), lambda i, j: (0, j)),
            pl.BlockSpec((D_MODEL, LANES), lambda i, j: (0, 0)),
        ],
        out_specs=[
            pl.BlockSpec((tm, tn), lambda i, j: (i, j)),
            pl.BlockSpec((tm, LANES), lambda i, j: (i, 0)),
        ],
        out_shape=[
            jax.ShapeDtypeStruct((R, N_MAIN), BF16),
            jax.ShapeDtypeStruct((R, LANES), F32),
        ],
        scratch_shapes=[pltpu.VMEM((tm, D_MODEL), BF16)],
        compiler_params=_cparams(("parallel", "arbitrary")),
        name="in_proj",
    )(hp, g, w_main, w_small)


def _mm3(a_parts, b_parts):
    ah, al = a_parts
    bh, bl = b_parts
    lhs = jnp.concatenate([ah, al, ah], axis=1)
    rhs = jnp.concatenate([bh, bh, bl], axis=0)
    return jnp.dot(lhs, rhs, preferred_element_type=F32)


def _unit_upper_inverses(at_list, eye):
    heads = range(len(at_list))
    sp = [_split2(a) for a in at_list]
    q = [_mm3(sp[h], sp[h]) for h in heads]
    y = [eye - a for a in at_list]
    for j in range(5):
        z = [jnp.concatenate([y[h], q[h]], axis=1) for h in heads]
        if j < INV_FULL_ROUNDS:
            zs = [_split2(v) for v in z]
            r = [_mm3((zs[h][0][:, BLK:], zs[h][1][:, BLK:]), zs[h]) for h in heads]
        else:
            qs = [_split2(v) for v in q]
            zb = [v.astype(BF16) for v in z]
            r = [jnp.dot(jnp.concatenate(qs[h], axis=1), jnp.concatenate([zb[h], zb[h]], axis=0),
                         preferred_element_type=F32) for h in heads]
        y = [y[h] + r[h][:, :BLK] for h in heads]
        q = [r[h][:, BLK:] for h in heads]
    qs = [_split2(v) for v in q]
    yb = [v.astype(BF16) for v in y]
    return [y[h] + jnp.dot(jnp.concatenate(qs[h], axis=1), jnp.concatenate([yb[h], yb[h]], axis=0),
                           preferred_element_type=F32) for h in heads]


def _gdn_kernel(x_ref, xp_ref, ps_ref, z_ref, cw_ref, alog_ref, dtb_ref, nw_ref, o_ref, s_ref):
    n = pl.program_id(1)
    heads = range(GDN_HEADS)

    @pl.when(n == 0)
    def _():
        s_ref[...] = jnp.zeros_like(s_ref)

    prev = xp_ref[0]
    x_bf = x_ref[0]
    xb = jnp.concatenate([jnp.where(n == 0, jnp.zeros_like(prev), prev), x_bf], axis=0)
    sr = lax.broadcasted_iota(I32, (BLK, PREV_ROWS + BLK), 0)
    sc = lax.broadcasted_iota(I32, (BLK, PREV_ROWS + BLK), 1)
    y = x_bf.astype(F32) * cw_ref[CONV_WIDTH - 1:CONV_WIDTH, :]
    for t in range(CONV_WIDTH - 1):
        shift = (sc == sr + (PREV_ROWS - (CONV_WIDTH - 1) + t)).astype(BF16)
        y = y + jnp.dot(shift, xb, preferred_element_type=F32) * cw_ref[t:t + 1, :]
    y = y * jax.nn.sigmoid(y)

    s = ps_ref[0]
    row = lax.broadcasted_iota(I32, (BLK, LANES), 0) + n * BLK
    lane = lax.broadcasted_iota(I32, (BLK, LANES), 1)
    xs = s + dtb_ref[...]
    softplus = jnp.maximum(xs, 0.0) + jnp.log(1.0 + jnp.exp(-jnp.abs(xs)))
    gb = jnp.where((lane >= S_DECAY) & (lane < S_DECAY + GDN_HEADS), -jnp.exp(alog_ref[...]) * softplus,
                   jnp.where((lane >= S_BETA) & (lane < S_BETA + GDN_HEADS), jax.nn.sigmoid(s), 0.0))
    gb = jnp.where(row >= PAD_ROWS, gb, 0.0)

    ri = lax.broadcasted_iota(I32, (BLK, BLK), 0)
    ci = lax.broadcasted_iota(I32, (BLK, BLK), 1)
    tri_incl = ri >= ci
    eye = (ri == ci).astype(F32)
    ones_l = tri_incl.astype(BF16)

    gh, gm, gl = _split3(gb)
    gc = jnp.dot(jnp.concatenate([ones_l, ones_l, ones_l], axis=1),
                 jnp.concatenate([gh, gm, gl], axis=0), preferred_element_type=F32)
    sel = (lax.broadcasted_iota(I32, (16, LANES), 1)
           == lax.broadcasted_iota(I32, (16, LANES), 0) + S_DECAY).astype(BF16)
    ch, cm, cl = _split3(gc)
    gct = _nt_dot(jnp.concatenate([sel, sel, sel], axis=1), jnp.concatenate([ch, cm, cl], axis=1))

    def head_cols(base, h):
        return y[:, base + h * GDN_DK:base + (h + 1) * GDN_DK]

    ones_sq = jnp.ones((GDN_DK, GDN_DK), BF16)

    def row_sums(sq):
        hi, lo = _split2(sq)
        return jnp.dot(jnp.concatenate([hi, lo], axis=1), jnp.concatenate([ones_sq, ones_sq], axis=0),
                       preferred_element_type=F32)

    def head_group(hs):
        q = {h: head_cols(C_GQ, h) for h in hs}
        k = {h: head_cols(C_GK, h) for h in hs}
        v = {h: head_cols(C_GV, h) for h in hs}
        q = {h: a * lax.rsqrt(row_sums(a * a) + EPS) * (GDN_DK ** -0.5) for h, a in q.items()}
        k = {h: a * lax.rsqrt(row_sums(a * a) + EPS) for h, a in k.items()}
        beta = {h: gb[:, S_BETA + h:S_BETA + h + 1] for h in hs}
        gcc = {h: gc[:, S_DECAY + h:S_DECAY + h + 1] for h in hs}
        g_last = {h: gcc[h][BLK - 1:BLK, :] for h in hs}
        decay = {h: jnp.where(tri_incl, jnp.exp(jnp.minimum(gcc[h] - gct[h:h + 1, :], 0.0)), 0.0) for h in hs}
        e_gc = {h: jnp.exp(gcc[h]) for h in hs}
        kb = {h: k[h] * beta[h] for h in hs}
        kbf = {h: k[h].astype(BF16) for h in hs}
        decay_t = {h: jnp.exp(jnp.minimum(gct[h:h + 1, :] - gcc[h], 0.0)) for h in hs}
        kk_t = {h: _nt_dot(kbf[h], kb[h].astype(BF16)) for h in hs}
        qk = {h: _nt_dot(q[h].astype(BF16), kbf[h]) for h in hs}
        tm_t = _unit_upper_inverses([jnp.where(ri < ci, kk_t[h] * decay_t[h], 0.0) for h in hs], eye)
        tm = {h: t.T for h, t in zip(hs, tm_t)}
        uw = {h: jnp.dot(tm[h].astype(BF16),
                         jnp.concatenate([v[h] * beta[h], kb[h] * e_gc[h]], axis=1).astype(BF16),
                         preferred_element_type=F32) for h in hs}
        attn = {h: (qk[h] * decay[h]).astype(BF16) for h in hs}
        q_dec = {h: (q[h] * e_gc[h]).astype(BF16) for h in hs}
        k_tail = {h: (k[h] * jnp.exp(g_last[h] - gcc[h])).T.astype(BF16) for h in hs}

        s_old = {h: s_ref[h] for h in hs}
        s_bf = {h: s_old[h].astype(BF16) for h in hs}
        v_new = {h: uw[h][:, :GDN_DV] - jnp.dot(uw[h][:, GDN_DV:].astype(BF16), s_bf[h], preferred_element_type=F32)
                 for h in hs}
        vn_bf = {h: v_new[h].astype(BF16) for h in hs}
        o = {h: jnp.dot(q_dec[h], s_bf[h], preferred_element_type=F32)
             + jnp.dot(attn[h], vn_bf[h], preferred_element_type=F32) for h in hs}
        for h in hs:
            s_ref[h] = s_old[h] * jnp.exp(g_last[h]) + jnp.dot(k_tail[h], vn_bf[h], preferred_element_type=F32)
        for h in hs:
            z = z_ref[0, :, h * GDN_DV:(h + 1) * GDN_DV].astype(F32)
            o_ref[0, :, h * GDN_DV:(h + 1) * GDN_DV] = (_rms(o[h], nw_ref[...]) * (z * jax.nn.sigmoid(z))).astype(o_ref.dtype)

    for g0 in range(0, GDN_HEADS, GDN_HEAD_GROUP):
        head_group(list(range(g0, g0 + GDN_HEAD_GROUP)))


def _gdn(P, Ps, conv_w, alog_v, dtb_v, norm_w):
    B, TP, _ = P.shape
    nb = TP // BLK
    L = TP - BLK
    wq = 3 * GDN_W
    const = lambda b, n: (0, 0)
    return pl.pallas_call(
        _gdn_kernel,
        grid=(B, nb),
        in_specs=[
            pl.BlockSpec((1, BLK, wq), lambda b, n: (b, n, 0)),
            pl.BlockSpec((1, PREV_ROWS, wq), lambda b, n: (b, jnp.maximum(n * (BLK // PREV_ROWS) - 1, 0), 0)),
            pl.BlockSpec((1, BLK, LANES), lambda b, n: (b, n, 0)),
            pl.BlockSpec((1, BLK, GDN_W), lambda b, n: (b, n, C_GZ // GDN_W)),
            pl.BlockSpec((CONV_WIDTH, wq), const),
            pl.BlockSpec((1, LANES), const),
            pl.BlockSpec((1, LANES), const),
            pl.BlockSpec((1, GDN_DV), const),
        ],
        out_specs=pl.BlockSpec((1, BLK, GDN_W), lambda b, n: (b, jnp.maximum(n - 1, 0), 0)),
        out_shape=jax.ShapeDtypeStruct((B, L, GDN_W), BF16),
        scratch_shapes=[pltpu.VMEM((GDN_HEADS, GDN_DK, GDN_DV), F32)],
        compiler_params=_cparams(("parallel", "arbitrary")),
        name="gdn_scan",
    )(P, P, Ps, P, conv_w, alog_v, dtb_v, norm_w)


def _rope128(x, cos, sin):
    return x * cos + pltpu.roll(x, ATT_HD // 2, 1) * sin


def _rope64(x, cos, sin, first_half):
    rot = jnp.where(first_half, pltpu.roll(x, LANES - IDX_HD // 2, 1), pltpu.roll(x, IDX_HD // 2, 1))
    return x * cos + rot * sin


def _dsa_prep_kernel(aq_ref, akv_ref, iq_ref, s_ref, ca_ref, sa_ref, ci_ref, si_ref,
                     qa_ref, ka_ref, vt_ref, qi_ref, ki_ref, wt_ref, *, n_real, q_scale):
    n = pl.program_id(1)
    for g in range(ATT_KV_HEADS):
        vt_ref[0, 0, g, ATT_HD:, :] = jnp.ones((SUBLANES, BLK), BF16)

    @pl.when(n >= n_real)
    def _():
        qa_ref[...] = jnp.zeros_like(qa_ref)
        ka_ref[...] = jnp.zeros_like(ka_ref)
        vt_ref[0, 0, :, :ATT_HD, :] = jnp.zeros((ATT_KV_HEADS, ATT_HD, BLK), BF16)
        qi_ref[...] = jnp.zeros_like(qi_ref)
        ki_ref[...] = jnp.zeros_like(ki_ref)
        wt_ref[...] = jnp.zeros_like(wt_ref)

    @pl.when(n < n_real)
    def _():
        ca, sa, ci, si = ca_ref[...], sa_ref[...], ci_ref[...], si_ref[...]
        lane = lax.broadcasted_iota(I32, (BLK, LANES), 1)
        first_half = (lane % IDX_HD) < (IDX_HD // 2)
        for h in range(ATT_HEADS):
            sl = slice(h * ATT_HD, (h + 1) * ATT_HD)
            qa_ref[0, :, sl] = (_rope128(aq_ref[0, :, sl].astype(F32), ca, sa) * q_scale).astype(BF16)
        for g in range(ATT_KV_HEADS):
            sl = slice(g * ATT_HD, (g + 1) * ATT_HD)
            ka_ref[0, :, sl] = _rope128(akv_ref[0, :, sl].astype(F32), ca, sa).astype(BF16)
            v = akv_ref[0, :, ATT_KV_W + g * ATT_HD:ATT_KV_W + (g + 1) * ATT_HD].astype(F32)
            vt_ref[0, 0, g, :ATT_HD, :] = v.T.astype(BF16)
        for p in range(IDX_Q_W // LANES):
            sl = slice(p * LANES, (p + 1) * LANES)
            qi_ref[0, :, sl] = _rope64(iq_ref[0, :, sl].astype(F32), ci, si, first_half).astype(BF16)
        s = s_ref[0]
        ki_ref[0] = _rope64(s, ci, si, first_half)[:, :IDX_HD].astype(BF16)
        wsc = s * ((IDX_HEADS * IDX_HD) ** -0.5)
        wh, wm, wl = _split3(wsc)
        sel = (lax.broadcasted_iota(I32, (16, LANES), 1)
               == lax.broadcasted_iota(I32, (16, LANES), 0) + S_IW).astype(BF16)
        wt = _nt_dot(jnp.concatenate([sel, sel, sel], axis=1), jnp.concatenate([wh, wm, wl], axis=1))
        wt_ref[0] = wt[:IDX_HEADS]


def _dsa_prep(P, Ps, cos_a, sin_a, cos_i, sin_i, nkb):
    B, TP, _ = P.shape
    n_real = TP // BLK
    TK = nkb * BLK
    cl = lambda b, n: (b, jnp.minimum(n, n_real - 1), 0)
    tab = lambda b, n: (jnp.minimum(n, n_real - 1), 0)
    return pl.pallas_call(
        functools.partial(_dsa_prep_kernel, n_real=n_real, q_scale=ATT_HD ** -0.5 * math.log2(math.e)),
        grid=(B, nkb),
        in_specs=[
            pl.BlockSpec((1, BLK, ATT_Q_W), lambda b, n: (b, jnp.minimum(n, n_real - 1), C_AQ // ATT_Q_W)),
            pl.BlockSpec((1, BLK, 2 * ATT_KV_W), lambda b, n: (b, jnp.minimum(n, n_real - 1), C_AK // (2 * ATT_KV_W))),
            pl.BlockSpec((1, BLK, IDX_Q_W), lambda b, n: (b, jnp.minimum(n, n_real - 1), C_IQ // IDX_Q_W)),
            pl.BlockSpec((1, BLK, LANES), cl),
            pl.BlockSpec((BLK, LANES), tab),
            pl.BlockSpec((BLK, LANES), tab),
            pl.BlockSpec((BLK, LANES), tab),
            pl.BlockSpec((BLK, LANES), tab),
        ],
        out_specs=[
            pl.BlockSpec((1, BLK, ATT_Q_W), lambda b, n: (b, n, 0)),
            pl.BlockSpec((1, BLK, ATT_KV_W), lambda b, n: (b, n, 0)),
            pl.BlockSpec((1, 1, ATT_KV_HEADS, ATT_HD + SUBLANES, BLK),
                         lambda b, n: (b, n // (WIDE_TILE // BLK), 0, 0, n % (WIDE_TILE // BLK))),
            pl.BlockSpec((1, BLK, IDX_Q_W), lambda b, n: (b, n, 0)),
            pl.BlockSpec((1, BLK, IDX_HD), lambda b, n: (b, n, 0)),
            pl.BlockSpec((1, IDX_HEADS, BLK), lambda b, n: (b, 0, n)),
        ],
        out_shape=[
            jax.ShapeDtypeStruct((B, TK, ATT_Q_W), BF16),
            jax.ShapeDtypeStruct((B, TK, ATT_KV_W), BF16),
            jax.ShapeDtypeStruct((B, TK // WIDE_TILE, ATT_KV_HEADS, ATT_HD + SUBLANES, WIDE_TILE), BF16),
            jax.ShapeDtypeStruct((B, TK, IDX_Q_W), BF16),
            jax.ShapeDtypeStruct((B, TK, IDX_HD), BF16),
            jax.ShapeDtypeStruct((B, IDX_HEADS, TK), F32),
        ],
        compiler_params=_cparams(("parallel", "arbitrary")),
        name="dsa_prep",
    )(P, P, P, Ps, cos_a, sin_a, cos_i, sin_i)


def _colsum(x):
    return jnp.sum(x, axis=0, keepdims=True, dtype=x.dtype)


def _fold8(x, op):
    return op(x.reshape(x.shape[0] // SUBLANES, SUBLANES, BLK), axis=0)


def _dsa_kernel(qa_ref, qi_ref, wt_ref, ka_ref, vt_ref, ki_ref, o_ref,
                s_ref, bias_ref, acc_ref, x_ref, *, topk):
    i = pl.program_id(1)
    q_row0 = (i + 1) * BLK
    nt = (i + 3) // 2
    kt = KEY_TILE
    qrow = lax.broadcasted_iota(I32, (kt, BLK), 1) + q_row0

    def rows(t):
        return pl.ds(pl.multiple_of(t * kt, kt), kt)

    qi = qi_ref[0]
    qs = jnp.concatenate([qi[:, h * IDX_HD:(h + 1) * IDX_HD] for h in range(IDX_HEADS)], axis=0)
    wt = wt_ref[0]
    per_w = WIDE_TILE // kt
    nw = (nt + per_w - 1) // per_w

    def score_tile(u, carry):
        lo8, hi8 = carry
        r0 = pl.multiple_of(u * WIDE_TILE, WIDE_TILE)
        d = _nt_dot(ki_ref[0, pl.ds(r0, WIDE_TILE), :], qs)
        for half in range(WIDE_TILE // kt):
            s = jnp.zeros((kt, BLK), F32)
            for h in range(IDX_HEADS):
                s = s + jnp.maximum(d[half * kt:(half + 1) * kt, h * BLK:(h + 1) * BLK], 0.0) * wt[h:h + 1, :]
            krow = lax.broadcasted_iota(I32, (kt, BLK), 0) + (r0 + half * kt)
            valid = (krow >= PAD_ROWS) & (krow <= qrow)
            s_ref[pl.ds(r0 + half * kt, kt), :] = jnp.where(valid, s, -jnp.inf)
            lo8 = jnp.minimum(lo8, _fold8(s, jnp.min))
            hi8 = jnp.maximum(hi8, _fold8(s, jnp.max))
        return lo8, hi8

    lo8, hi8 = lax.fori_loop(0, nw, score_tile,
                             (jnp.full((SUBLANES, BLK), jnp.inf, F32), jnp.full((SUBLANES, BLK), -jnp.inf, F32)))

    def count_pass(p):
        def body(t, acc):
            ge = s_ref[pl.ds(pl.multiple_of(t * COUNT_TILE, COUNT_TILE), COUNT_TILE), :] >= p
            return acc + _fold8(jnp.where(ge, 1, 0).astype(I32), functools.partial(jnp.sum, dtype=I32))
        n_count = (nt * kt + COUNT_TILE - 1) // COUNT_TILE
        return _colsum(lax.fori_loop(0, n_count, body, jnp.zeros((SUBLANES, BLK), I32)))

    def snap_pass(p):
        def body(t, carry):
            acc, up, dn = carry
            s = s_ref[rows(t), :]
            ge = s >= p
            acc = acc + _fold8(jnp.where(ge, 1, 0).astype(I32), functools.partial(jnp.sum, dtype=I32))
            up = jnp.minimum(up, _fold8(jnp.where(ge, s, jnp.inf), jnp.min))
            dn = jnp.maximum(dn, _fold8(jnp.where(ge, -jnp.inf, s), jnp.max))
            return acc, up, dn
        acc, up, dn = lax.fori_loop(0, nt, body, (jnp.zeros((SUBLANES, BLK), I32),
                                                  jnp.full((SUBLANES, BLK), jnp.inf, F32),
                                                  jnp.full((SUBLANES, BLK), -jnp.inf, F32)))
        return _colsum(acc), jnp.min(up, axis=0, keepdims=True), jnp.max(dn, axis=0, keepdims=True)

    def n_active(clo, lo, hi):
        return jnp.max(jnp.where((clo > topk) & (lo < hi), 1, 0).astype(I32))

    lo = jnp.min(lo8, axis=0, keepdims=True)
    hi = jnp.max(hi8, axis=0, keepdims=True)
    clo = lax.broadcasted_iota(I32, (1, BLK), 1) + (q_row0 - PAD_ROWS + 1)
    chi = jnp.zeros((1, BLK), I32)

    def probe(state):
        lo, hi, clo, chi = state
        p = 0.5 * lo + 0.5 * hi
        cnt = count_pass(p)
        act = (clo > topk) & (p > lo) & (p < hi)
        up = act & (cnt >= topk)
        dn = act & (cnt < topk)
        return jnp.where(up, p, lo), jnp.where(dn, p, hi), jnp.where(up, cnt, clo), jnp.where(dn, cnt, chi)

    lo, hi, clo, chi = lax.fori_loop(0, FAST_PROBES, lambda _, st: probe(st), (lo, hi, clo, chi))
    unsettled = n_active(clo, lo, hi)

    def plain_bias(thr):
        def body(t, carry):
            bias_ref[rows(t), :] = jnp.where(s_ref[rows(t), :] >= thr, 0.0, NEG_BIG)
            return carry
        lax.fori_loop(0, per_w * nw, body, 0)

    @pl.when(unsettled == 0)
    def _():
        plain_bias(lo)

    @pl.when(unsettled != 0)
    def _():
        def slow_cond(c):
            return n_active(c[2], c[0], c[1]) > 0

        def slow_body(c):
            lo, hi, clo, chi = c
            mid = 0.5 * lo + 0.5 * hi
            p = jnp.where(mid > lo, jnp.minimum(mid, hi), hi)
            cnt, nxt, prv = snap_pass(p)
            act = (clo > topk) & (lo < hi)
            up = act & (cnt >= topk)
            dn = act & (cnt < topk)
            return (jnp.where(up, nxt, lo), jnp.where(dn, prv, hi), jnp.where(up, cnt, clo), jnp.where(dn, cnt, chi))

        thr, _, c_ge, c_gt = lax.while_loop(slow_cond, slow_body, (lo, hi, clo, chi))
        need = (topk - c_gt).astype(F32)
        any_tie = jnp.max(jnp.where(c_ge > topk, 1, 0).astype(I32))

        def tie_bias(t, run):
            s = s_ref[rows(t), :]
            eq = (s == thr) & (c_ge > topk)
            lower = (lax.broadcasted_iota(I32, (kt, kt), 0) >= lax.broadcasted_iota(I32, (kt, kt), 1)).astype(BF16)
            e = jnp.where(eq, 1.0, 0.0)
            rank = jnp.dot(lower, e.astype(BF16), preferred_element_type=F32) + run
            keep = (eq & (rank <= need)) | (jnp.logical_not(eq) & (s >= thr))
            bias_ref[rows(t), :] = jnp.where(keep, 0.0, NEG_BIG)
            return run + jnp.sum(e, axis=0, keepdims=True)

        @pl.when(any_tie == 0)
        def _():
            plain_bias(thr)

        @pl.when(any_tie != 0)
        def _():
            lax.fori_loop(0, per_w * nw, tie_bias, jnp.zeros((1, BLK), F32))

    gw = ATT_GROUP * BLK
    acc_ref[...] = jnp.zeros_like(acc_ref)

    def att_tile(u, m):
        r0 = pl.multiple_of(u * WIDE_TILE, WIDE_TILE)
        b = bias_ref[pl.ds(r0, WIDE_TILE), :]
        b4 = jnp.concatenate([b] * ATT_GROUP, axis=1)
        for g in range(ATT_KV_HEADS):
            qg = jnp.concatenate([qa_ref[0, :, (g * ATT_GROUP + r) * ATT_HD:(g * ATT_GROUP + r + 1) * ATT_HD]
                                  for r in range(ATT_GROUP)], axis=0)
            x_ref[g] = _nt_dot(ka_ref[0, pl.ds(r0, WIDE_TILE), g * ATT_HD:(g + 1) * ATT_HD], qg)
        m_out = []
        for g in range(ATT_KV_HEADS):
            x = x_ref[g] + b4
            m_new = jnp.maximum(m[g], jnp.max(x, axis=0, keepdims=True))
            alpha = jnp.exp2(m[g] - m_new)
            p = jnp.exp2(x - m_new).astype(BF16)
            acc_ref[g] = acc_ref[g] * alpha + jnp.dot(vt_ref[0, u, g], p, preferred_element_type=F32)
            m_out.append(m_new)
        return tuple(m_out)

    lax.fori_loop(0, nw, att_tile, tuple(jnp.full((1, gw), NEG_BIG, F32) for _ in range(ATT_KV_HEADS)))
    for g in range(ATT_KV_HEADS):
        acc = acc_ref[g]
        out_t = acc[:ATT_HD] / acc[ATT_HD:ATT_HD + 1]
        for r in range(ATT_GROUP):
            hh = g * ATT_GROUP + r
            o_ref[0, :, hh * ATT_HD:(hh + 1) * ATT_HD] = out_t[:, r * BLK:(r + 1) * BLK].T.astype(o_ref.dtype)


def _dsa(qa, qi, wt, ka, vt, ki, L, topk):
    B, TK, _ = qa.shape
    nq = L // BLK
    return pl.pallas_call(
        functools.partial(_dsa_kernel, topk=topk),
        grid=(B, nq),
        in_specs=[
            pl.BlockSpec((1, BLK, ATT_Q_W), lambda b, i: (b, i + 1, 0)),
            pl.BlockSpec((1, BLK, IDX_Q_W), lambda b, i: (b, i + 1, 0)),
            pl.BlockSpec((1, IDX_HEADS, BLK), lambda b, i: (b, 0, i + 1)),
            pl.BlockSpec((1, TK, ATT_KV_W), lambda b, i: (b, 0, 0)),
            pl.BlockSpec((1, TK // WIDE_TILE, ATT_KV_HEADS, ATT_HD + SUBLANES, WIDE_TILE), lambda b, i: (b, 0, 0, 0, 0)),
            pl.BlockSpec((1, TK, IDX_HD), lambda b, i: (b, 0, 0)),
        ],
        out_specs=pl.BlockSpec((1, BLK, ATT_Q_W), lambda b, i: (b, i, 0)),
        out_shape=jax.ShapeDtypeStruct((B, L, ATT_Q_W), BF16),
        scratch_shapes=[pltpu.VMEM((TK, BLK), F32), pltpu.VMEM((TK, BLK), F32),
                        pltpu.VMEM((ATT_KV_HEADS, ATT_HD + SUBLANES, ATT_GROUP * BLK), F32),
                        pltpu.VMEM((ATT_KV_HEADS, WIDE_TILE, ATT_GROUP * BLK), F32)],
        compiler_params=_cparams(("parallel", "arbitrary")),
        name="dsa_attn",
    )(qa, qi, wt, ka, vt, ki)


def _merge_kernel(*refs):
    x_ref, yg_ref, yd_ref = refs[:3]
    gate_refs = refs[3:3 + 2 * MERGE_BLOCKS]
    wg_ref, wd_ref, wo_ref, npost_ref, npre_ref, h_ref, n2_ref = refs[3 + 2 * MERGE_BLOCKS:]
    gate = lambda blocks: jnp.concatenate([jax.nn.sigmoid(r[...].astype(F32)) for r in blocks], axis=0)
    bg = jnp.dot(yg_ref[...], wg_ref[...], preferred_element_type=F32)
    bd = jnp.dot(yd_ref[...], wd_ref[...], preferred_element_type=F32)
    merged = gate(gate_refs[:MERGE_BLOCKS]) * bg + gate(gate_refs[MERGE_BLOCKS:]) * bd
    mix = jnp.dot(merged.astype(BF16), wo_ref[...], preferred_element_type=F32)
    h1 = x_ref[...] + _rms(mix, npost_ref[...])
    h_ref[...] = h1
    n2_ref[...] = _rms(h1, npre_ref[...]).astype(BF16)


def _merge(x2, yg, yd, P2, wg, wd, wo, npost, npre, L, TP):
    R = x2.shape[0]
    tm = MERGE_BLOCKS * BLK
    assert L % tm == 0
    per = L // BLK
    def gate_spec(col, k):
        def index(i):
            blk = i * MERGE_BLOCKS + k
            return ((blk // per) * (TP // BLK) + 1 + blk % per, col // D_MODEL)
        return pl.BlockSpec((BLK, D_MODEL), index)
    row = lambda i: (i, 0)
    full = lambda i: (0, 0)
    gates = [gate_spec(C_GA, k) for k in range(MERGE_BLOCKS)] + [gate_spec(C_GB, k) for k in range(MERGE_BLOCKS)]
    return pl.pallas_call(
        _merge_kernel,
        grid=(R // tm,),
        in_specs=[pl.BlockSpec((tm, D_MODEL), row), pl.BlockSpec((tm, GDN_W), row), pl.BlockSpec((tm, ATT_Q_W), row)]
        + gates
        + [pl.BlockSpec((GDN_W, D_MODEL), full), pl.BlockSpec((ATT_Q_W, D_MODEL), full),
           pl.BlockSpec((D_MODEL, D_MODEL), full), pl.BlockSpec((1, D_MODEL), full), pl.BlockSpec((1, D_MODEL), full)],
        out_specs=[pl.BlockSpec((tm, D_MODEL), row), pl.BlockSpec((tm, D_MODEL), row)],
        out_shape=[jax.ShapeDtypeStruct((R, D_MODEL), F32), jax.ShapeDtypeStruct((R, D_MODEL), BF16)],
        compiler_params=_cparams(("parallel",)),
        name="merge_out",
    )(x2, yg, yd, *([P2] * (2 * MERGE_BLOCKS)), wg, wd, wo, npost, npre)


def _mlp_kernel(n_ref, h_ref, wu_ref, wd_ref, g_ref, o_ref, acc_ref):
    j = pl.program_id(1)

    @pl.when(j == 0)
    def _():
        acc_ref[...] = jnp.zeros_like(acc_ref)

    u = jnp.maximum(jnp.dot(n_ref[...], wu_ref[...], preferred_element_type=F32), 0.0)
    acc_ref[...] += jnp.dot((u * u).astype(BF16), wd_ref[...], preferred_element_type=F32)

    @pl.when(j == pl.num_programs(1) - 1)
    def _():
        o_ref[...] = h_ref[...] + _rms(acc_ref[...], g_ref[...])


def _mlp(n2, h1, wu, wd, g, tm, tf):
    R = n2.shape[0]
    return pl.pallas_call(
        _mlp_kernel,
        grid=(R // tm, D_FF // tf),
        in_specs=[
            pl.BlockSpec((tm, D_MODEL), lambda i, j: (i, 0)),
            pl.BlockSpec((tm, D_MODEL), lambda i, j: (i, 0)),
            pl.BlockSpec((D_MODEL, tf), lambda i, j: (0, j)),
            pl.BlockSpec((tf, D_MODEL), lambda i, j: (j, 0)),
            pl.BlockSpec((1, D_MODEL), lambda i, j: (0, 0)),
        ],
        out_specs=pl.BlockSpec((tm, D_MODEL), lambda i, j: (i, 0)),
        out_shape=jax.ShapeDtypeStruct((R, D_MODEL), F32),
        scratch_shapes=[pltpu.VMEM((tm, D_MODEL), F32)],
        compiler_params=_cparams(("parallel", "arbitrary")),
        name="mlp",
    )(n2, h1, wu, wd, g)


def _largest_tile(n, candidates):
    for c in candidates:
        if n % c == 0:
            return c
    raise ValueError(f"no tile for {n}")


def _rope_tables(tp, tk):
    pos = jnp.maximum(jnp.arange(tk, dtype=jnp.int32) - PAD_ROWS, 0)
    pos = jnp.where(jnp.arange(tk) < tp, pos, 0).astype(F32)

    def table(dim):
        inv = ROPE_THETA ** (-jnp.arange(0, dim, 2, dtype=F32) / dim)
        ang = pos[:, None] * inv[None, :]
        c, s = jnp.cos(ang), jnp.sin(ang)
        reps = LANES // dim
        return jnp.tile(jnp.concatenate([c, c], axis=1), (1, reps)), jnp.tile(jnp.concatenate([-s, s], axis=1), (1, reps))

    return table(ATT_HD) + table(IDX_HD)


def kernel(x, meta_tokens, pre_mix_norm, w_in, conv_w, a_log, dt_bias, gdn_norm, w_branch_gdn,
           w_branch_dsa, w_out, post_mix_norm, pre_mlp_norm, w_up, w_down, post_mlp_norm):
    B, L, D = x.shape
    assert D == D_MODEL and L % BLK == 0 and w_in.shape[0] == 1, "single layer, 128-aligned sequence"
    TP = BLK + L
    per_wide = WIDE_TILE // BLK
    nkb = -(-(TP // BLK) // per_wide) * per_wide
    topk = min(TOPK_MAX, L // 4)

    w = w_in[0]
    o = np.cumsum((0, GDN_W, GDN_W, GDN_W, GDN_W, GDN_HEADS, GDN_HEADS, ATT_Q_W, ATT_KV_W, ATT_KV_W,
                   IDX_Q_W, IDX_HD, IDX_HEADS, D_MODEL, D_MODEL))
    seg = lambda i: w[:, o[i]:o[i + 1]]
    w_main = jnp.concatenate([seg(0), seg(1), seg(2), seg(3), seg(6), seg(7), seg(8), seg(9), seg(12), seg(13)],
                             axis=1).astype(BF16)
    w_small = jnp.concatenate([seg(10), seg(11), seg(4), seg(5),
                               jnp.zeros((D, LANES - IDX_HD - IDX_HEADS - 2 * GDN_HEADS), F32)], axis=1)
    lane_vec = lambda v, off: jnp.zeros((1, LANES), F32).at[0, off:off + v.shape[0]].set(v.astype(F32))
    alog_v = lane_vec(a_log[0], S_DECAY)
    dtb_v = lane_vec(dt_bias[0], S_DECAY)

    hp = jnp.concatenate([jnp.zeros((B, PAD_ROWS, D), x.dtype),
                          jnp.broadcast_to(meta_tokens.astype(x.dtype)[None], (B, N_META, D)), x], axis=1)
    R = B * TP
    P2, Ps2 = _proj(hp.reshape(R, D), pre_mix_norm[0][None], w_main, w_small,
                    _largest_tile(R, PROJ_ROW_TILES), PROJ_COL_TILE)
    P = P2.reshape(B, TP, N_MAIN)
    Ps = Ps2.reshape(B, TP, LANES)

    y_gdn = _gdn(P, Ps, conv_w[0], alog_v, dtb_v, gdn_norm[0][None])

    cos_a, sin_a, cos_i, sin_i = _rope_tables(TP, nkb * BLK)
    qa, ka, vt, qi, ki, wt = _dsa_prep(P, Ps, cos_a, sin_a, cos_i, sin_i, nkb)
    y_dsa = _dsa(qa, qi, wt, ka, vt, ki, L, topk)

    Rr = B * L
    h1, n2 = _merge(x.reshape(Rr, D), y_gdn.reshape(Rr, GDN_W), y_dsa.reshape(Rr, ATT_Q_W), P2,
                    w_branch_gdn[0].astype(BF16), w_branch_dsa[0].astype(BF16), w_out[0].astype(BF16),
                    post_mix_norm[0][None], pre_mlp_norm[0][None], L, TP)
    out = _mlp(n2, h1, w_up[0].astype(BF16), w_down[0].astype(BF16), post_mlp_norm[0][None],
               _largest_tile(Rr, MLP_ROW_TILES), MLP_FF_TILE)
    return out.reshape(B, L, D)
```

```python
import functools
import math

import jax
import jax.numpy as jnp
import numpy as np
from jax import lax
from jax.experimental import pallas as pl
from jax.experimental.pallas import tpu as pltpu

F32 = jnp.float32
BF16 = jnp.bfloat16
I32 = jnp.int32

D_MODEL = 1024
N_META = 16
GDN_HEADS = 8
GDN_DK = 128
GDN_DV = 128
CONV_WIDTH = 4
ATT_HEADS = 8
ATT_KV_HEADS = 2
ATT_HD = 128
ATT_GROUP = ATT_HEADS // ATT_KV_HEADS
IDX_HEADS = 8
IDX_HD = 64
TOPK_MAX = 256
ROPE_THETA = 10000.0
D_FF = 4 * D_MODEL
EPS = 1e-6

GDN_W = GDN_HEADS * GDN_DK
ATT_Q_W = ATT_HEADS * ATT_HD
ATT_KV_W = ATT_KV_HEADS * ATT_HD
IDX_Q_W = IDX_HEADS * IDX_HD

LANES = 128
SUBLANES = 8
BLK = 128
PAD_ROWS = BLK - N_META
KEY_TILE = 256
COUNT_TILE = 512
WIDE_TILE = 1024
VMEM_LIMIT = 56 * 1024 * 1024
PROJ_ROW_TILES = (1024, 768, 512, 384, 256, 128)
PROJ_COL_TILE = 2048
MLP_ROW_TILES = (1024, 512, 256, 128)
MLP_FF_TILE = 1024
INV_FULL_ROUNDS = 3
GDN_BATCH = 2
MERGE_BLOCKS = 4
PREV_ROWS = 16

C_GQ, C_GK, C_GV, C_GZ = 0, 1024, 2048, 3072
C_AQ, C_AK, C_AV, C_IQ = 4096, 5120, 5376, 5632
C_GA, C_GB = 6144, 7168
N_MAIN = 8192
S_IK, S_IW, S_BETA, S_DECAY = 0, 64, 72, 80

NEG_BIG = -1e30
FAST_PROBES = 20


def _cparams(sem):
    return pltpu.CompilerParams(dimension_semantics=sem, vmem_limit_bytes=VMEM_LIMIT)


def _split3(a):
    h = a.astype(BF16)
    r = a - h.astype(F32)
    m = r.astype(BF16)
    l = (r - m.astype(F32)).astype(BF16)
    return h, m, l


def _split2(a):
    h = a.astype(BF16)
    return h, (a - h.astype(F32)).astype(BF16)


def _rms(x, g):
    return x * lax.rsqrt(jnp.mean(x * x, axis=-1, keepdims=True) + EPS) * g


def _nt_dot(a, b):
    return lax.dot_general(a, b, (((1,), (1,)), ((), ())), preferred_element_type=F32)


def _proj_kernel(h_ref, g_ref, w_ref, ws_ref, o_ref, os_ref, n_ref):
    j = pl.program_id(1)

    @pl.when(j == 0)
    def _():
        n = _rms(h_ref[...], g_ref[...])
        n_ref[...] = n.astype(BF16)
        nh, nl = _split2(n)
        wh, wl = _split2(ws_ref[...])
        os_ref[...] = (jnp.dot(nh, wh, preferred_element_type=F32)
                       + jnp.dot(nl, wh, preferred_element_type=F32)
                       + jnp.dot(nh, wl, preferred_element_type=F32))

    o_ref[...] = jnp.dot(n_ref[...], w_ref[...], preferred_element_type=F32).astype(o_ref.dtype)


def _proj(hp, g, w_main, w_small, tm, tn):
    R = hp.shape[0]
    return pl.pallas_call(
        _proj_kernel,
        grid=(R // tm, N_MAIN // tn),
        in_specs=[
            pl.BlockSpec((tm, D_MODEL), lambda i, j: (i, 0)),
            pl.BlockSpec((1, D_MODEL), lambda i, j: (0, 0)),
            pl.BlockSpec((D_MODEL, tn), lambda i, j: (0, j)),
            pl.BlockSpec((D_MODEL, LANES), lambda i, j: (0, 0)),
        ],
        out_specs=[
            pl.BlockSpec((tm, tn), lambda i, j: (i, j)),
            pl.BlockSpec((tm, LANES), lambda i, j: (i, 0)),
        ],
        out_shape=[
            jax.ShapeDtypeStruct((R, N_MAIN), BF16),
            jax.ShapeDtypeStruct((R, LANES), F32),
        ],
        scratch_shapes=[pltpu.VMEM((tm, D_MODEL), BF16)],
        compiler_params=_cparams(("parallel", "arbitrary")),
        name="in_proj",
    )(hp, g, w_main, w_small)


def _mm3(a_parts, b_parts):
    ah, al = a_parts
    bh, bl = b_parts
    lhs = jnp.concatenate([ah, al, ah], axis=1)
    rhs = jnp.concatenate([bh, bh, bl], axis=0)
    return jnp.dot(lhs, rhs, preferred_element_type=F32)


def _unit_upper_inverses(at_list, eye):
    heads = range(len(at_list))
    sp = [_split2(a) for a in at_list]
    q = [_mm3(sp[h], sp[h]) for h in heads]
    y = [eye - a for a in at_list]
    for j in range(5):
        z = [jnp.concatenate([y[h], q[h]], axis=1) for h in heads]
        if j < INV_FULL_ROUNDS:
            zs = [_split2(v) for v in z]
            r = [_mm3((zs[h][0][:, BLK:], zs[h][1][:, BLK:]), zs[h]) for h in heads]
        else:
            qs = [_split2(v) for v in q]
            zb = [v.astype(BF16) for v in z]
            r = [jnp.dot(jnp.concatenate(qs[h], axis=1), jnp.concatenate([zb[h], zb[h]], axis=0),
                         preferred_element_type=F32) for h in heads]
        y = [y[h] + r[h][:, :BLK] for h in heads]
        q = [r[h][:, BLK:] for h in heads]
    qs = [_split2(v) for v in q]
    yb = [v.astype(BF16) for v in y]
    return [y[h] + jnp.dot(jnp.concatenate(qs[h], axis=1), jnp.concatenate([yb[h], yb[h]], axis=0),
                           preferred_element_type=F32) for h in heads]


def _gdn_kernel(x_ref, xp_ref, ps_ref, z_ref, cw_ref, alog_ref, dtb_ref, nw_ref, o_ref, s_ref, *, nbatch):
    n = pl.program_id(1)

    @pl.when(n == 0)
    def _():
        s_ref[...] = jnp.zeros_like(s_ref)

    sr = lax.broadcasted_iota(I32, (BLK, PREV_ROWS + BLK), 0)
    sc = lax.broadcasted_iota(I32, (BLK, PREV_ROWS + BLK), 1)
    row = lax.broadcasted_iota(I32, (BLK, LANES), 0) + n * BLK
    lane = lax.broadcasted_iota(I32, (BLK, LANES), 1)
    ri = lax.broadcasted_iota(I32, (BLK, BLK), 0)
    ci = lax.broadcasted_iota(I32, (BLK, BLK), 1)
    tri_incl = ri >= ci
    eye = (ri == ci).astype(F32)
    ones_l = tri_incl.astype(BF16)
    sel = (lax.broadcasted_iota(I32, (16, LANES), 1)
           == lax.broadcasted_iota(I32, (16, LANES), 0) + S_DECAY).astype(BF16)
    ones_sq = jnp.ones((GDN_DK, GDN_DK), BF16)

    ys, gbs, gcs, gcts = [], [], [], []
    for bi in range(nbatch):
        prev = xp_ref[bi]
        x_bf = x_ref[bi]
        xb = jnp.concatenate([jnp.where(n == 0, jnp.zeros_like(prev), prev), x_bf], axis=0)
        y = x_bf.astype(F32) * cw_ref[CONV_WIDTH - 1:CONV_WIDTH, :]
        for t in range(CONV_WIDTH - 1):
            shift = (sc == sr + (PREV_ROWS - (CONV_WIDTH - 1) + t)).astype(BF16)
            y = y + jnp.dot(shift, xb, preferred_element_type=F32) * cw_ref[t:t + 1, :]
        ys.append(y * jax.nn.sigmoid(y))

        s = ps_ref[bi]
        xs = s + dtb_ref[...]
        softplus = jnp.maximum(xs, 0.0) + jnp.log(1.0 + jnp.exp(-jnp.abs(xs)))
        gb = jnp.where((lane >= S_DECAY) & (lane < S_DECAY + GDN_HEADS), -jnp.exp(alog_ref[...]) * softplus,
                       jnp.where((lane >= S_BETA) & (lane < S_BETA + GDN_HEADS), jax.nn.sigmoid(s), 0.0))
        gb = jnp.where(row >= PAD_ROWS, gb, 0.0)
        gh, gm, gl = _split3(gb)
        gc = jnp.dot(jnp.concatenate([ones_l, ones_l, ones_l], axis=1),
                     jnp.concatenate([gh, gm, gl], axis=0), preferred_element_type=F32)
        ch, cm, cl = _split3(gc)
        gbs.append(gb)
        gcs.append(gc)
        gcts.append(_nt_dot(jnp.concatenate([sel, sel, sel], axis=1), jnp.concatenate([ch, cm, cl], axis=1)))

    def head_cols(base, key):
        bi, h = key
        return ys[bi][:, base + h * GDN_DK:base + (h + 1) * GDN_DK]

    def row_sums(sq):
        hi, lo = _split2(sq)
        return jnp.dot(jnp.concatenate([hi, lo], axis=1), jnp.concatenate([ones_sq, ones_sq], axis=0),
                       preferred_element_type=F32)

    hs = [(bi, h) for bi in range(nbatch) for h in range(GDN_HEADS)]
    q = {c: head_cols(C_GQ, c) for c in hs}
    k = {c: head_cols(C_GK, c) for c in hs}
    v = {c: head_cols(C_GV, c) for c in hs}
    q = {c: a * lax.rsqrt(row_sums(a * a) + EPS) * (GDN_DK ** -0.5) for c, a in q.items()}
    k = {c: a * lax.rsqrt(row_sums(a * a) + EPS) for c, a in k.items()}
    beta = {(bi, h): gbs[bi][:, S_BETA + h:S_BETA + h + 1] for bi, h in hs}
    gcc = {(bi, h): gcs[bi][:, S_DECAY + h:S_DECAY + h + 1] for bi, h in hs}
    gcr = {(bi, h): gcts[bi][h:h + 1, :] for bi, h in hs}
    g_last = {c: gcc[c][BLK - 1:BLK, :] for c in hs}
    decay = {c: jnp.where(tri_incl, jnp.exp(jnp.minimum(gcc[c] - gcr[c], 0.0)), 0.0) for c in hs}
    e_gc = {c: jnp.exp(gcc[c]) for c in hs}
    kb = {c: k[c] * beta[c] for c in hs}
    kbf = {c: k[c].astype(BF16) for c in hs}
    decay_t = {c: jnp.exp(jnp.minimum(gcr[c] - gcc[c], 0.0)) for c in hs}
    kk_t = {c: _nt_dot(kbf[c], kb[c].astype(BF16)) for c in hs}
    qk = {c: _nt_dot(q[c].astype(BF16), kbf[c]) for c in hs}
    tm_t = _unit_upper_inverses([jnp.where(ri < ci, kk_t[c] * decay_t[c], 0.0) for c in hs], eye)
    tm = {c: t.T for c, t in zip(hs, tm_t)}
    uw = {c: jnp.dot(tm[c].astype(BF16),
                     jnp.concatenate([v[c] * beta[c], kb[c] * e_gc[c]], axis=1).astype(BF16),
                     preferred_element_type=F32) for c in hs}
    attn = {c: (qk[c] * decay[c]).astype(BF16) for c in hs}
    q_dec = {c: (q[c] * e_gc[c]).astype(BF16) for c in hs}
    k_tail = {c: (k[c] * jnp.exp(g_last[c] - gcc[c])).T.astype(BF16) for c in hs}

    s_old = {c: s_ref[c[0], c[1]] for c in hs}
    s_bf = {c: s_old[c].astype(BF16) for c in hs}
    v_new = {c: uw[c][:, :GDN_DV] - jnp.dot(uw[c][:, GDN_DV:].astype(BF16), s_bf[c], preferred_element_type=F32)
             for c in hs}
    vn_bf = {c: v_new[c].astype(BF16) for c in hs}
    o = {c: jnp.dot(q_dec[c], s_bf[c], preferred_element_type=F32)
         + jnp.dot(attn[c], vn_bf[c], preferred_element_type=F32) for c in hs}
    for c in hs:
        s_ref[c[0], c[1]] = s_old[c] * jnp.exp(g_last[c]) + jnp.dot(k_tail[c], vn_bf[c], preferred_element_type=F32)
    for bi, h in hs:
        z = z_ref[bi, :, h * GDN_DV:(h + 1) * GDN_DV].astype(F32)
        o_ref[bi, :, h * GDN_DV:(h + 1) * GDN_DV] = (
            _rms(o[(bi, h)], nw_ref[...]) * (z * jax.nn.sigmoid(z))).astype(o_ref.dtype)


def _gdn(P, Ps, conv_w, alog_v, dtb_v, norm_w):
    B, TP, _ = P.shape
    nb = TP // BLK
    L = TP - BLK
    wq = 3 * GDN_W
    nbatch = GDN_BATCH if B % GDN_BATCH == 0 else 1
    const = lambda b, n: (0, 0)
    return pl.pallas_call(
        functools.partial(_gdn_kernel, nbatch=nbatch),
        grid=(B // nbatch, nb),
        in_specs=[
            pl.BlockSpec((nbatch, BLK, wq), lambda b, n: (b, n, 0)),
            pl.BlockSpec((nbatch, PREV_ROWS, wq), lambda b, n: (b, jnp.maximum(n * (BLK // PREV_ROWS) - 1, 0), 0)),
            pl.BlockSpec((nbatch, BLK, LANES), lambda b, n: (b, n, 0)),
            pl.BlockSpec((nbatch, BLK, GDN_W), lambda b, n: (b, n, C_GZ // GDN_W)),
            pl.BlockSpec((CONV_WIDTH, wq), const),
            pl.BlockSpec((1, LANES), const),
            pl.BlockSpec((1, LANES), const),
            pl.BlockSpec((1, GDN_DV), const),
        ],
        out_specs=pl.BlockSpec((nbatch, BLK, GDN_W), lambda b, n: (b, jnp.maximum(n - 1, 0), 0)),
        out_shape=jax.ShapeDtypeStruct((B, L, GDN_W), BF16),
        scratch_shapes=[pltpu.VMEM((nbatch, GDN_HEADS, GDN_DK, GDN_DV), F32)],
        compiler_params=_cparams(("parallel", "arbitrary")),
        name="gdn_scan",
    )(P, P, Ps, P, conv_w, alog_v, dtb_v, norm_w)


def _rope128(x, cos, sin):
    return x * cos + pltpu.roll(x, ATT_HD // 2, 1) * sin


def _rope64(x, cos, sin, first_half):
    rot = jnp.where(first_half, pltpu.roll(x, LANES - IDX_HD // 2, 1), pltpu.roll(x, IDX_HD // 2, 1))
    return x * cos + rot * sin


def _dsa_prep_kernel(aq_ref, akv_ref, iq_ref, s_ref, ca_ref, sa_ref, ci_ref, si_ref,
                     qa_ref, ka_ref, vt_ref, qi_ref, ki_ref, wt_ref, *, n_real, q_scale):
    n = pl.program_id(1)
    for g in range(ATT_KV_HEADS):
        vt_ref[0, 0, g, ATT_HD:, :] = jnp.ones((SUBLANES, BLK), BF16)

    @pl.when(n >= n_real)
    def _():
        qa_ref[...] = jnp.zeros_like(qa_ref)
        ka_ref[...] = jnp.zeros_like(ka_ref)
        vt_ref[0, 0, :, :ATT_HD, :] = jnp.zeros((ATT_KV_HEADS, ATT_HD, BLK), BF16)
        qi_ref[...] = jnp.zeros_like(qi_ref)
        ki_ref[...] = jnp.zeros_like(ki_ref)
        wt_ref[...] = jnp.zeros_like(wt_ref)

    @pl.when(n < n_real)
    def _():
        ca, sa, ci, si = ca_ref[...], sa_ref[...], ci_ref[...], si_ref[...]
        lane = lax.broadcasted_iota(I32, (BLK, LANES), 1)
        first_half = (lane % IDX_HD) < (IDX_HD // 2)
        for h in range(ATT_HEADS):
            sl = slice(h * ATT_HD, (h + 1) * ATT_HD)
            qa_ref[0, :, sl] = (_rope128(aq_ref[0, :, sl].astype(F32), ca, sa) * q_scale).astype(BF16)
        for g in range(ATT_KV_HEADS):
            sl = slice(g * ATT_HD, (g + 1) * ATT_HD)
            ka_ref[0, :, sl] = _rope128(akv_ref[0, :, sl].astype(F32), ca, sa).astype(BF16)
            v = akv_ref[0, :, ATT_KV_W + g * ATT_HD:ATT_KV_W + (g + 1) * ATT_HD].astype(F32)
            vt_ref[0, 0, g, :ATT_HD, :] = v.T.astype(BF16)
        for p in range(IDX_Q_W // LANES):
            sl = slice(p * LANES, (p + 1) * LANES)
            qi_ref[0, :, sl] = _rope64(iq_ref[0, :, sl].astype(F32), ci, si, first_half).astype(BF16)
        s = s_ref[0]
        ki_ref[0] = _rope64(s, ci, si, first_half)[:, :IDX_HD].astype(BF16)
        wsc = s * ((IDX_HEADS * IDX_HD) ** -0.5)
        wh, wm, wl = _split3(wsc)
        sel = (lax.broadcasted_iota(I32, (16, LANES), 1)
               == lax.broadcasted_iota(I32, (16, LANES), 0) + S_IW).astype(BF16)
        wt = _nt_dot(jnp.concatenate([sel, sel, sel], axis=1), jnp.concatenate([wh, wm, wl], axis=1))
        wt_ref[0] = wt[:IDX_HEADS]


def _dsa_prep(P, Ps, cos_a, sin_a, cos_i, sin_i, nkb):
    B, TP, _ = P.shape
    n_real = TP // BLK
    TK = nkb * BLK
    cl = lambda b, n: (b, jnp.minimum(n, n_real - 1), 0)
    tab = lambda b, n: (jnp.minimum(n, n_real - 1), 0)
    return pl.pallas_call(
        functools.partial(_dsa_prep_kernel, n_real=n_real, q_scale=ATT_HD ** -0.5 * math.log2(math.e)),
        grid=(B, nkb),
        in_specs=[
            pl.BlockSpec((1, BLK, ATT_Q_W), lambda b, n: (b, jnp.minimum(n, n_real - 1), C_AQ // ATT_Q_W)),
            pl.BlockSpec((1, BLK, 2 * ATT_KV_W), lambda b, n: (b, jnp.minimum(n, n_real - 1), C_AK // (2 * ATT_KV_W))),
            pl.BlockSpec((1, BLK, IDX_Q_W), lambda b, n: (b, jnp.minimum(n, n_real - 1), C_IQ // IDX_Q_W)),
            pl.BlockSpec((1, BLK, LANES), cl),
            pl.BlockSpec((BLK, LANES), tab),
            pl.BlockSpec((BLK, LANES), tab),
            pl.BlockSpec((BLK, LANES), tab),
            pl.BlockSpec((BLK, LANES), tab),
        ],
        out_specs=[
            pl.BlockSpec((1, BLK, ATT_Q_W), lambda b, n: (b, n, 0)),
            pl.BlockSpec((1, BLK, ATT_KV_W), lambda b, n: (b, n, 0)),
            pl.BlockSpec((1, 1, ATT_KV_HEADS, ATT_HD + SUBLANES, BLK),
                         lambda b, n: (b, n // (WIDE_TILE // BLK), 0, 0, n % (WIDE_TILE // BLK))),
            pl.BlockSpec((1, BLK, IDX_Q_W), lambda b, n: (b, n, 0)),
            pl.BlockSpec((1, BLK, IDX_HD), lambda b, n: (b, n, 0)),
            pl.BlockSpec((1, IDX_HEADS, BLK), lambda b, n: (b, 0, n)),
        ],
        out_shape=[
            jax.ShapeDtypeStruct((B, TK, ATT_Q_W), BF16),
            jax.ShapeDtypeStruct((B, TK, ATT_KV_W), BF16),
            jax.ShapeDtypeStruct((B, TK // WIDE_TILE, ATT_KV_HEADS, ATT_HD + SUBLANES, WIDE_TILE), BF16),
            jax.ShapeDtypeStruct((B, TK, IDX_Q_W), BF16),
            jax.ShapeDtypeStruct((B, TK, IDX_HD), BF16),
            jax.ShapeDtypeStruct((B, IDX_HEADS, TK), F32),
        ],
        compiler_params=_cparams(("parallel", "arbitrary")),
        name="dsa_prep",
    )(P, P, P, Ps, cos_a, sin_a, cos_i, sin_i)


def _colsum(x):
    return jnp.sum(x, axis=0, keepdims=True, dtype=x.dtype)


def _fold8(x, op):
    return op(x.reshape(x.shape[0] // SUBLANES, SUBLANES, BLK), axis=0)


def _dsa_kernel(qa_ref, qi_ref, wt_ref, ka_ref, vt_ref, ki_ref, o_ref,
                s_ref, bias_ref, acc_ref, x_ref, *, topk):
    i = pl.program_id(1)
    q_row0 = (i + 1) * BLK
    nt = (i + 3) // 2
    kt = KEY_TILE
    qrow = lax.broadcasted_iota(I32, (kt, BLK), 1) + q_row0

    def rows(t):
        return pl.ds(pl.multiple_of(t * kt, kt), kt)

    qi = qi_ref[0]
    qs = jnp.concatenate([qi[:, h * IDX_HD:(h + 1) * IDX_HD] for h in range(IDX_HEADS)], axis=0)
    wt = wt_ref[0]
    per_w = WIDE_TILE // kt
    nw = (nt + per_w - 1) // per_w

    def score_tile(u, carry):
        lo8, hi8 = carry
        r0 = pl.multiple_of(u * WIDE_TILE, WIDE_TILE)
        d = _nt_dot(ki_ref[0, pl.ds(r0, WIDE_TILE), :], qs)
        for half in range(WIDE_TILE // kt):
            s = jnp.zeros((kt, BLK), F32)
            for h in range(IDX_HEADS):
                s = s + jnp.maximum(d[half * kt:(half + 1) * kt, h * BLK:(h + 1) * BLK], 0.0) * wt[h:h + 1, :]
            krow = lax.broadcasted_iota(I32, (kt, BLK), 0) + (r0 + half * kt)
            valid = (krow >= PAD_ROWS) & (krow <= qrow)
            s_ref[pl.ds(r0 + half * kt, kt), :] = jnp.where(valid, s, -jnp.inf)
            lo8 = jnp.minimum(lo8, _fold8(s, jnp.min))
            hi8 = jnp.maximum(hi8, _fold8(s, jnp.max))
        return lo8, hi8

    lo8, hi8 = lax.fori_loop(0, nw, score_tile,
                             (jnp.full((SUBLANES, BLK), jnp.inf, F32), jnp.full((SUBLANES, BLK), -jnp.inf, F32)))

    def count_pass(p):
        def body(t, acc):
            ge = s_ref[pl.ds(pl.multiple_of(t * COUNT_TILE, COUNT_TILE), COUNT_TILE), :] >= p
            return acc + _fold8(jnp.where(ge, 1, 0).astype(I32), functools.partial(jnp.sum, dtype=I32))
        n_count = (nt * kt + COUNT_TILE - 1) // COUNT_TILE
        return _colsum(lax.fori_loop(0, n_count, body, jnp.zeros((SUBLANES, BLK), I32)))

    def snap_pass(p):
        def body(t, carry):
            acc, up, dn = carry
            s = s_ref[rows(t), :]
            ge = s >= p
            acc = acc + _fold8(jnp.where(ge, 1, 0).astype(I32), functools.partial(jnp.sum, dtype=I32))
            up = jnp.minimum(up, _fold8(jnp.where(ge, s, jnp.inf), jnp.min))
            dn = jnp.maximum(dn, _fold8(jnp.where(ge, -jnp.inf, s), jnp.max))
            return acc, up, dn
        acc, up, dn = lax.fori_loop(0, nt, body, (jnp.zeros((SUBLANES, BLK), I32),
                                                  jnp.full((SUBLANES, BLK), jnp.inf, F32),
                                                  jnp.full((SUBLANES, BLK), -jnp.inf, F32)))
        return _colsum(acc), jnp.min(up, axis=0, keepdims=True), jnp.max(dn, axis=0, keepdims=True)

    def n_active(clo, lo, hi):
        return jnp.max(jnp.where((clo > topk) & (lo < hi), 1, 0).astype(I32))

    lo = jnp.min(lo8, axis=0, keepdims=True)
    hi = jnp.max(hi8, axis=0, keepdims=True)
    clo = lax.broadcasted_iota(I32, (1, BLK), 1) + (q_row0 - PAD_ROWS + 1)
    chi = jnp.zeros((1, BLK), I32)

    def probe(state):
        lo, hi, clo, chi = state
        p = 0.5 * lo + 0.5 * hi
        cnt = count_pass(p)
        act = (clo > topk) & (p > lo) & (p < hi)
        up = act & (cnt >= topk)
        dn = act & (cnt < topk)
        return jnp.where(up, p, lo), jnp.where(dn, p, hi), jnp.where(up, cnt, clo), jnp.where(dn, cnt, chi)

    lo, hi, clo, chi = lax.fori_loop(0, FAST_PROBES, lambda _, st: probe(st), (lo, hi, clo, chi))
    unsettled = n_active(clo, lo, hi)

    def plain_bias(thr):
        def body(t, carry):
            bias_ref[rows(t), :] = jnp.where(s_ref[rows(t), :] >= thr, 0.0, NEG_BIG)
            return carry
        lax.fori_loop(0, per_w * nw, body, 0)

    @pl.when(unsettled == 0)
    def _():
        plain_bias(lo)

    @pl.when(unsettled != 0)
    def _():
        def slow_cond(c):
            return n_active(c[2], c[0], c[1]) > 0

        def slow_body(c):
            lo, hi, clo, chi = c
            mid = 0.5 * lo + 0.5 * hi
            p = jnp.where(mid > lo, jnp.minimum(mid, hi), hi)
            cnt, nxt, prv = snap_pass(p)
            act = (clo > topk) & (lo < hi)
            up = act & (cnt >= topk)
            dn = act & (cnt < topk)
            return (jnp.where(up, nxt, lo), jnp.where(dn, prv, hi), jnp.where(up, cnt, clo), jnp.where(dn, cnt, chi))

        thr, _, c_ge, c_gt = lax.while_loop(slow_cond, slow_body, (lo, hi, clo, chi))
        need = (topk - c_gt).astype(F32)
        any_tie = jnp.max(jnp.where(c_ge > topk, 1, 0).astype(I32))

        def tie_bias(t, run):
            s = s_ref[rows(t), :]
            eq = (s == thr) & (c_ge > topk)
            lower = (lax.broadcasted_iota(I32, (kt, kt), 0) >= lax.broadcasted_iota(I32, (kt, kt), 1)).astype(BF16)
            e = jnp.where(eq, 1.0, 0.0)
            rank = jnp.dot(lower, e.astype(BF16), preferred_element_type=F32) + run
            keep = (eq & (rank <= need)) | (jnp.logical_not(eq) & (s >= thr))
            bias_ref[rows(t), :] = jnp.where(keep, 0.0, NEG_BIG)
            return run + jnp.sum(e, axis=0, keepdims=True)

        @pl.when(any_tie == 0)
        def _():
            plain_bias(thr)

        @pl.when(any_tie != 0)
        def _():
            lax.fori_loop(0, per_w * nw, tie_bias, jnp.zeros((1, BLK), F32))

    gw = ATT_GROUP * BLK
    acc_ref[...] = jnp.zeros_like(acc_ref)

    def att_tile(u, m):
        r0 = pl.multiple_of(u * WIDE_TILE, WIDE_TILE)
        b = bias_ref[pl.ds(r0, WIDE_TILE), :]
        b4 = jnp.concatenate([b] * ATT_GROUP, axis=1)
        for g in range(ATT_KV_HEADS):
            qg = jnp.concatenate([qa_ref[0, :, (g * ATT_GROUP + r) * ATT_HD:(g * ATT_GROUP + r + 1) * ATT_HD]
                                  for r in range(ATT_GROUP)], axis=0)
            x_ref[g] = _nt_dot(ka_ref[0, pl.ds(r0, WIDE_TILE), g * ATT_HD:(g + 1) * ATT_HD], qg)
        m_out = []
        for g in range(ATT_KV_HEADS):
            x = x_ref[g] + b4
            m_new = jnp.maximum(m[g], jnp.max(x, axis=0, keepdims=True))
            alpha = jnp.exp2(m[g] - m_new)
            p = jnp.exp2(x - m_new).astype(BF16)
            acc_ref[g] = acc_ref[g] * alpha + jnp.dot(vt_ref[0, u, g], p, preferred_element_type=F32)
            m_out.append(m_new)
        return tuple(m_out)

    lax.fori_loop(0, nw, att_tile, tuple(jnp.full((1, gw), NEG_BIG, F32) for _ in range(ATT_KV_HEADS)))
    for g in range(ATT_KV_HEADS):
        acc = acc_ref[g]
        out_t = acc[:ATT_HD] / acc[ATT_HD:ATT_HD + 1]
        for r in range(ATT_GROUP):
            hh = g * ATT_GROUP + r
            o_ref[0, :, hh * ATT_HD:(hh + 1) * ATT_HD] = out_t[:, r * BLK:(r + 1) * BLK].T.astype(o_ref.dtype)


def _dsa(qa, qi, wt, ka, vt, ki, L, topk):
    B, TK, _ = qa.shape
    nq = L // BLK
    return pl.pallas_call(
        functools.partial(_dsa_kernel, topk=topk),
        grid=(B, nq),
        in_specs=[
            pl.BlockSpec((1, BLK, ATT_Q_W), lambda b, i: (b, i + 1, 0)),
            pl.BlockSpec((1, BLK, IDX_Q_W), lambda b, i: (b, i + 1, 0)),
            pl.BlockSpec((1, IDX_HEADS, BLK), lambda b, i: (b, 0, i + 1)),
            pl.BlockSpec((1, TK, ATT_KV_W), lambda b, i: (b, 0, 0)),
            pl.BlockSpec((1, TK // WIDE_TILE, ATT_KV_HEADS, ATT_HD + SUBLANES, WIDE_TILE), lambda b, i: (b, 0, 0, 0, 0)),
            pl.BlockSpec((1, TK, IDX_HD), lambda b, i: (b, 0, 0)),
        ],
        out_specs=pl.BlockSpec((1, BLK, ATT_Q_W), lambda b, i: (b, i, 0)),
        out_shape=jax.ShapeDtypeStruct((B, L, ATT_Q_W), BF16),
        scratch_shapes=[pltpu.VMEM((TK, BLK), F32), pltpu.VMEM((TK, BLK), F32),
                        pltpu.VMEM((ATT_KV_HEADS, ATT_HD + SUBLANES, ATT_GROUP * BLK), F32),
                        pltpu.VMEM((ATT_KV_HEADS, WIDE_TILE, ATT_GROUP * BLK), F32)],
        compiler_params=_cparams(("parallel", "arbitrary")),
        name="dsa_attn",
    )(qa, qi, wt, ka, vt, ki)


def _merge_kernel(*refs):
    x_ref, yg_ref, yd_ref = refs[:3]
    gate_refs = refs[3:3 + 2 * MERGE_BLOCKS]
    wg_ref, wd_ref, wo_ref, npost_ref, npre_ref, h_ref, n2_ref = refs[3 + 2 * MERGE_BLOCKS:]
    gate = lambda blocks: jnp.concatenate([jax.nn.sigmoid(r[...].astype(F32)) for r in blocks], axis=0)
    bg = jnp.dot(yg_ref[...], wg_ref[...], preferred_element_type=F32)
    bd = jnp.dot(yd_ref[...], wd_ref[...], preferred_element_type=F32)
    merged = gate(gate_refs[:MERGE_BLOCKS]) * bg + gate(gate_refs[MERGE_BLOCKS:]) * bd
    mix = jnp.dot(merged.astype(BF16), wo_ref[...], preferred_element_type=F32)
    h1 = x_ref[...] + _rms(mix, npost_ref[...])
    h_ref[...] = h1
    n2_ref[...] = _rms(h1, npre_ref[...]).astype(BF16)


def _merge(x2, yg, yd, P2, wg, wd, wo, npost, npre, L, TP):
    R = x2.shape[0]
    tm = MERGE_BLOCKS * BLK
    assert L % tm == 0
    per = L // BLK
    def gate_spec(col, k):
        def index(i):
            blk = i * MERGE_BLOCKS + k
            return ((blk // per) * (TP // BLK) + 1 + blk % per, col // D_MODEL)
        return pl.BlockSpec((BLK, D_MODEL), index)
    row = lambda i: (i, 0)
    full = lambda i: (0, 0)
    gates = [gate_spec(C_GA, k) for k in range(MERGE_BLOCKS)] + [gate_spec(C_GB, k) for k in range(MERGE_BLOCKS)]
    return pl.pallas_call(
        _merge_kernel,
        grid=(R // tm,),
        in_specs=[pl.BlockSpec((tm, D_MODEL), row), pl.BlockSpec((tm, GDN_W), row), pl.BlockSpec((tm, ATT_Q_W), row)]
        + gates
        + [pl.BlockSpec((GDN_W, D_MODEL), full), pl.BlockSpec((ATT_Q_W, D_MODEL), full),
           pl.BlockSpec((D_MODEL, D_MODEL), full), pl.BlockSpec((1, D_MODEL), full), pl.BlockSpec((1, D_MODEL), full)],
        out_specs=[pl.BlockSpec((tm, D_MODEL), row), pl.BlockSpec((tm, D_MODEL), row)],
        out_shape=[jax.ShapeDtypeStruct((R, D_MODEL), F32), jax.ShapeDtypeStruct((R, D_MODEL), BF16)],
        compiler_params=_cparams(("parallel",)),
        name="merge_out",
    )(x2, yg, yd, *([P2] * (2 * MERGE_BLOCKS)), wg, wd, wo, npost, npre)


def _mlp_kernel(n_ref, h_ref, wu_ref, wd_ref, g_ref, o_ref, acc_ref):
    j = pl.program_id(1)

    @pl.when(j == 0)
    def _():
        acc_ref[...] = jnp.zeros_like(acc_ref)

    u = jnp.maximum(jnp.dot(n_ref[...], wu_ref[...], preferred_element_type=F32), 0.0)
    acc_ref[...] += jnp.dot((u * u).astype(BF16), wd_ref[...], preferred_element_type=F32)

    @pl.when(j == pl.num_programs(1) - 1)
    def _():
        o_ref[...] = h_ref[...] + _rms(acc_ref[...], g_ref[...])


def _mlp(n2, h1, wu, wd, g, tm, tf):
    R = n2.shape[0]
    return pl.pallas_call(
        _mlp_kernel,
        grid=(R // tm, D_FF // tf),
        in_specs=[
            pl.BlockSpec((tm, D_MODEL), lambda i, j: (i, 0)),
            pl.BlockSpec((tm, D_MODEL), lambda i, j: (i, 0)),
            pl.BlockSpec((D_MODEL, tf), lambda i, j: (0, j)),
            pl.BlockSpec((tf, D_MODEL), lambda i, j: (j, 0)),
            pl.BlockSpec((1, D_MODEL), lambda i, j: (0, 0)),
        ],
        out_specs=pl.BlockSpec((tm, D_MODEL), lambda i, j: (i, 0)),
        out_shape=jax.ShapeDtypeStruct((R, D_MODEL), F32),
        scratch_shapes=[pltpu.VMEM((tm, D_MODEL), F32)],
        compiler_params=_cparams(("parallel", "arbitrary")),
        name="mlp",
    )(n2, h1, wu, wd, g)


def _largest_tile(n, candidates):
    for c in candidates:
        if n % c == 0:
            return c
    raise ValueError(f"no tile for {n}")


def _rope_tables(tp, tk):
    pos = jnp.maximum(jnp.arange(tk, dtype=jnp.int32) - PAD_ROWS, 0)
    pos = jnp.where(jnp.arange(tk) < tp, pos, 0).astype(F32)

    def table(dim):
        inv = ROPE_THETA ** (-jnp.arange(0, dim, 2, dtype=F32) / dim)
        ang = pos[:, None] * inv[None, :]
        c, s = jnp.cos(ang), jnp.sin(ang)
        reps = LANES // dim
        return jnp.tile(jnp.concatenate([c, c], axis=1), (1, reps)), jnp.tile(jnp.concatenate([-s, s], axis=1), (1, reps))

    return table(ATT_HD) + table(IDX_HD)


def kernel(x, meta_tokens, pre_mix_norm, w_in, conv_w, a_log, dt_bias, gdn_norm, w_branch_gdn,
           w_branch_dsa, w_out, post_mix_norm, pre_mlp_norm, w_up, w_down, post_mlp_norm):
    B, L, D = x.shape
    assert D == D_MODEL and L % BLK == 0 and w_in.shape[0] == 1, "single layer, 128-aligned sequence"
    TP = BLK + L
    per_wide = WIDE_TILE // BLK
    nkb = -(-(TP // BLK) // per_wide) * per_wide
    topk = min(TOPK_MAX, L // 4)

    w = w_in[0]
    o = np.cumsum((0, GDN_W, GDN_W, GDN_W, GDN_W, GDN_HEADS, GDN_HEADS, ATT_Q_W, ATT_KV_W, ATT_KV_W,
                   IDX_Q_W, IDX_HD, IDX_HEADS, D_MODEL, D_MODEL))
    seg = lambda i: w[:, o[i]:o[i + 1]]
    w_main = jnp.concatenate([seg(0), seg(1), seg(2), seg(3), seg(6), seg(7), seg(8), seg(9), seg(12), seg(13)],
                             axis=1).astype(BF16)
    w_small = jnp.concatenate([seg(10), seg(11), seg(4), seg(5),
                               jnp.zeros((D, LANES - IDX_HD - IDX_HEADS - 2 * GDN_HEADS), F32)], axis=1)
    lane_vec = lambda v, off: jnp.zeros((1, LANES), F32).at[0, off:off + v.shape[0]].set(v.astype(F32))
    alog_v = lane_vec(a_log[0], S_DECAY)
    dtb_v = lane_vec(dt_bias[0], S_DECAY)

    hp = jnp.concatenate([jnp.zeros((B, PAD_ROWS, D), x.dtype),
                          jnp.broadcast_to(meta_tokens.astype(x.dtype)[None], (B, N_META, D)), x], axis=1)
    R = B * TP
    P2, Ps2 = _proj(hp.reshape(R, D), pre_mix_norm[0][None], w_main, w_small,
                    _largest_tile(R, PROJ_ROW_TILES), PROJ_COL_TILE)
    P = P2.reshape(B, TP, N_MAIN)
    Ps = Ps2.reshape(B, TP, LANES)

    y_gdn = _gdn(P, Ps, conv_w[0], alog_v, dtb_v, gdn_norm[0][None])

    cos_a, sin_a, cos_i, sin_i = _rope_tables(TP, nkb * BLK)
    qa, ka, vt, qi, ki, wt = _dsa_prep(P, Ps, cos_a, sin_a, cos_i, sin_i, nkb)
    y_dsa = _dsa(qa, qi, wt, ka, vt, ki, L, topk)

    Rr = B * L
    h1, n2 = _merge(x.reshape(Rr, D), y_gdn.reshape(Rr, GDN_W), y_dsa.reshape(Rr, ATT_Q_W), P2,
                    w_branch_gdn[0].astype(BF16), w_branch_dsa[0].astype(BF16), w_out[0].astype(BF16),
                    post_mix_norm[0][None], pre_mlp_norm[0][None], L, TP)
    out = _mlp(n2, h1, w_up[0].astype(BF16), w_down[0].astype(BF16), post_mlp_norm[0][None],
               _largest_tile(Rr, MLP_ROW_TILES), MLP_FF_TILE)
    return out.reshape(B, L, D)
```
